```python
import math
import jax, jax.numpy as jnp
from jax import lax
import numpy as np

D_MODEL = 1024
BATCH = 4
SEQ = 4096
DEPTH = 1
DEC_BATCH = 128
DEC_SEQ = 1
PAST_LEN = 8192
PAGE_SIZE = 128

N_META = 16
D_MIX = D_MODEL
D_ATTN = D_MIX // 2
D_CONV = D_MIX - D_ATTN
N_HEADS = 4
D_VHEAD = D_ATTN // N_HEADS
D_HEAD = D_VHEAD // 2
ROT_DIM = D_HEAD // 4
ROPE_THETA = 500000.0
CONV_WIDTH = 31
N_EXPERTS = 32
TOP_K = 4
D_EXPERT = D_MODEL
SWIGLU_LIMIT = 7.0
SWIGLU_ALPHA = 1.702
MOE_BLOCK = 128
Q_BLOCK = 128
RMS_EPS = 1e-6
LN_EPS = 1e-5
Q_W = N_HEADS * 2 * D_HEAD
K_W = N_HEADS * 2 * D_HEAD
V_W = N_HEADS * D_VHEAD
C_W = 2 * D_CONV
D_IN = Q_W + K_W + V_W + C_W

kernel_name = "hymba_diffattn_conformer_moe_step"

F32 = jnp.float32


def _rmsnorm(x, g, eps=RMS_EPS):
    xf = x.astype(F32)
    y = xf * lax.rsqrt(jnp.mean(xf * xf, axis=-1, keepdims=True) + eps)
    return (y * g.astype(F32)).astype(x.dtype)


def _layernorm(x, g, b):
    xf = x.astype(F32)
    mu = jnp.mean(xf, axis=-1, keepdims=True)
    var = jnp.mean(jnp.square(xf - mu), axis=-1, keepdims=True)
    y = (xf - mu) * lax.rsqrt(var + LN_EPS) * g.astype(F32) + b.astype(F32)
    return y.astype(x.dtype)


def _lambda_init(layer):
    return 0.8 - 0.6 * math.exp(-0.3 * layer)


def _diff_lambda(lq1, lk1, lq2, lk2, layer):
    a = jnp.sum(lq1.astype(F32) * lk1.astype(F32))
    b = jnp.sum(lq2.astype(F32) * lk2.astype(F32))
    return jnp.exp(a) - jnp.exp(b) + _lambda_init(layer)


def _rope(x, pos):
    inv = ROPE_THETA ** (-jnp.arange(0, ROT_DIM, 2, dtype=F32) / ROT_DIM)
    ang = pos.astype(F32)[:, None] * inv[None, :]
    cos = jnp.cos(ang)[None, :, None, None, :]
    sin = jnp.sin(ang)[None, :, None, None, :]
    xr = x[..., :ROT_DIM].astype(F32)
    x1, x2 = xr[..., :ROT_DIM // 2], xr[..., ROT_DIM // 2:]
    rot = jnp.concatenate([x1 * cos - x2 * sin, x2 * cos + x1 * sin], axis=-1)
    return jnp.concatenate([rot.astype(x.dtype), x[..., ROT_DIM:]], axis=-1)


def _project(h, w_in, pos):
    B, S, _ = h.shape
    z = h @ w_in
    q, k, v, c = jnp.split(z, [Q_W, Q_W + K_W, Q_W + K_W + V_W], axis=-1)
    q = _rope(q.reshape(B, S, N_HEADS, 2, D_HEAD), pos)
    k = _rope(k.reshape(B, S, N_HEADS, 2, D_HEAD), pos)
    v = v.reshape(B, S, N_HEADS, D_VHEAD)
    a, gate = jnp.split(c, 2, axis=-1)
    glu = a * jax.nn.sigmoid(gate)
    return q, k, v, glu


def _scores(q, k):
    return jnp.einsum('bqhcd,bkhcd->bhcqk', q.astype(F32), k.astype(F32)) * (D_HEAD ** -0.5)


def _diff_combine(s, v, lam):
    p = jax.nn.softmax(s, axis=-1)
    o = jnp.einsum('bhcqk,bkhe->bhcqe', p, v.astype(F32))
    return jnp.swapaxes(o[:, :, 0] - lam * o[:, :, 1], 1, 2)


def _attn_prompt(q, k, v, lam):
    B, L = q.shape[:2]
    pos = jnp.arange(L)
    s = _scores(q[:, :N_META], k[:, :N_META])
    mask = pos[:N_META, None] >= pos[None, :N_META]
    o_meta = _diff_combine(jnp.where(mask, s, -jnp.inf), v[:, :N_META], lam)
    n_blk = (L - N_META) // Q_BLOCK
    q_blk = jnp.moveaxis(q[:, N_META:].reshape(B, n_blk, Q_BLOCK, N_HEADS, 2, D_HEAD), 1, 0)

    def one_block(args):
        qb, j = args
        qpos = N_META + j * Q_BLOCK + jnp.arange(Q_BLOCK)
        sb = _scores(qb, k)
        sb = jnp.where(qpos[:, None] >= pos[None, :], sb, -jnp.inf)
        return _diff_combine(sb, v, lam)

    o = lax.map(one_block, (q_blk, jnp.arange(n_blk)))
    o = jnp.moveaxis(o, 0, 1).reshape(B, L - N_META, N_HEADS, D_VHEAD)
    return jnp.concatenate([o_meta, o], axis=1)


def _online_update(carry, s, v):
    m, l, acc = carry
    m_new = jnp.maximum(m, jnp.max(s, axis=-1))
    corr = jnp.exp(m - m_new)
    p = jnp.exp(s - m_new[..., None])
    l = l * corr + jnp.sum(p, axis=-1)
    acc = acc * corr[..., None] + jnp.einsum('bhcqk,bkhe->bhcqe', p, v.astype(F32))
    return (m_new, l, acc)


def _attn_sample(q, k_new, v_new, cache_k, cache_v, page_table, layer, pos, lam):
    Bd, S = q.shape[:2]
    qf = q.astype(F32) * (D_HEAD ** -0.5)
    init = (jnp.full((Bd, N_HEADS, 2, S), -jnp.inf, F32),
            jnp.zeros((Bd, N_HEADS, 2, S), F32),
            jnp.zeros((Bd, N_HEADS, 2, S, D_VHEAD), F32))

    def page_step(carry, pages):
        kp = cache_k[layer, pages]
        vp = cache_v[layer, pages]
        s = jnp.einsum('bqhcd,bkhcd->bhcqk', qf, kp.astype(F32))
        return _online_update(carry, s, vp), None

    carry, _ = lax.scan(page_step, init, page_table.T)
    s_new = jnp.einsum('bqhcd,bkhcd->bhcqk', qf, k_new.astype(F32))
    s_new = jnp.where(pos[:, None] >= pos[None, :], s_new, -jnp.inf)
    m, l, acc = _online_update(carry, s_new, v_new)
    o = acc / l[..., None]
    return jnp.swapaxes(o[:, :, 0] - lam * o[:, :, 1], 1, 2)


def _attn_post(o, subln_g, layer):
    B, S = o.shape[:2]
    o = _rmsnorm(o, subln_g) * (1.0 - _lambda_init(layer))
    return o.reshape(B, S, D_ATTN)


def _conv_branch(buf, w, b, ln_g, ln_b):
    y = lax.conv_general_dilated(buf, w[:, None, :].astype(buf.dtype), window_strides=(1,),
                                 padding='VALID', dimension_numbers=('NWC', 'WIO', 'NWC'),
                                 feature_group_count=buf.shape[-1])
    y = y + b.astype(y.dtype)
    return jax.nn.silu(_layernorm(y, ln_g, ln_b))


def _moe(h, w_router, b_router, w_gate, b_gate, w_up, b_up, w_down, b_down):
    T = h.shape[0]
    logits = h.astype(F32) @ w_router.astype(F32) + b_router.astype(F32)
    top_val, top_idx = lax.top_k(logits, TOP_K)
    gates = jax.nn.softmax(top_val, axis=-1)
    A = T * TOP_K
    flat_e = top_idx.reshape(-1)
    flat_tok = jnp.arange(A, dtype=jnp.int32) // TOP_K
    flat_g = gates.reshape(-1)
    order = jnp.argsort(flat_e)
    sorted_e = flat_e[order]
    counts = jnp.bincount(flat_e, length=N_EXPERTS)
    padded = ((counts + MOE_BLOCK - 1) // MOE_BLOCK) * MOE_BLOCK
    ends = jnp.cumsum(padded)
    pad_start = ends - padded
    grp_start = jnp.cumsum(counts) - counts
    dest = pad_start[sorted_e] + jnp.arange(A) - grp_start[sorted_e]
    n_blocks = (A + MOE_BLOCK - 1) // MOE_BLOCK + N_EXPERTS
    P = n_blocks * MOE_BLOCK
    slot_tok = jnp.full((P,), T, jnp.int32).at[dest].set(flat_tok[order])
    slot_gate = jnp.zeros((P,), F32).at[dest].set(flat_g[order])
    block_e = jnp.minimum(jnp.searchsorted(ends, jnp.arange(n_blocks) * MOE_BLOCK, side='right'),
                          N_EXPERTS - 1)
    h_pad = jnp.concatenate([h, jnp.zeros((1, h.shape[1]), h.dtype)], axis=0)
    xs = h_pad[slot_tok].reshape(n_blocks, MOE_BLOCK, h.shape[1])

    def expert_block(args):
        xb, e = args
        g = jnp.minimum(xb @ w_gate[e] + b_gate[e], SWIGLU_LIMIT)
        u = jnp.clip(xb @ w_up[e] + b_up[e], -SWIGLU_LIMIT, SWIGLU_LIMIT)
        a = g * jax.nn.sigmoid(SWIGLU_ALPHA * g) * (u + 1.0)
        return a @ w_down[e] + b_down[e]

    ys = lax.map(expert_block, (xs, block_e)).reshape(P, -1).astype(F32) * slot_gate[:, None]
    out = jax.ops.segment_sum(ys, slot_tok, num_segments=T + 1)[:T]
    return out.astype(h.dtype)


def setup_inputs(seed: int = 0) -> dict:
    key = jax.random.key(seed)
    ks = iter(jax.random.split(key, 40))
    nrm = lambda shape, scale: scale * jax.random.normal(next(ks), shape, F32)
    n_pages = PAST_LEN // PAGE_SIZE
    n_used = DEC_BATCH * n_pages
    n_phys = n_used + max(1, n_used // 4)
    page_table = jax.random.permutation(next(ks), n_phys)[:n_used].reshape(DEC_BATCH, n_pages).astype(jnp.int32)
    return {
        "x_prompt": nrm((BATCH, SEQ, D_MODEL), 1.0),
        "x_sample": nrm((DEC_BATCH, DEC_SEQ, D_MODEL), 1.0),
        "cache_k": nrm((DEPTH, n_phys, PAGE_SIZE, N_HEADS, 2, D_HEAD), 1.0),
        "cache_v": nrm((DEPTH, n_phys, PAGE_SIZE, N_HEADS, D_VHEAD), 1.0),
        "page_table": page_table,
        "state_conv": nrm((DEPTH, DEC_BATCH, CONV_WIDTH - 1, D_CONV), 0.5),
        "meta_tokens": nrm((N_META, D_MODEL), 1.0),
        "norm1_g": 1.0 + nrm((DEPTH, D_MODEL), 0.02),
        "w_in": nrm((DEPTH, D_MODEL, D_IN), D_MODEL ** -0.5),
        "lambda_q1": nrm((DEPTH, D_HEAD), 0.1),
        "lambda_k1": nrm((DEPTH, D_HEAD), 0.1),
        "lambda_q2": nrm((DEPTH, D_HEAD), 0.1),
        "lambda_k2": nrm((DEPTH, D_HEAD), 0.1),
        "subln_g": 1.0 + nrm((DEPTH, D_VHEAD), 0.02),
        "conv_w": nrm((DEPTH, CONV_WIDTH, D_CONV), CONV_WIDTH ** -0.5),
        "conv_b": nrm((DEPTH, D_CONV), 0.02),
        "conv_ln_g": 1.0 + nrm((DEPTH, D_CONV), 0.02),
        "conv_ln_b": nrm((DEPTH, D_CONV), 0.02),
        "w_out": nrm((DEPTH, D_MIX, D_MODEL), D_MIX ** -0.5),
        "norm2_g": 1.0 + nrm((DEPTH, D_MODEL), 0.02),
        "w_router": nrm((DEPTH, D_MODEL, N_EXPERTS), D_MODEL ** -0.5),
        "b_router": nrm((DEPTH, N_EXPERTS), 0.01),
        "w_gate": nrm((DEPTH, N_EXPERTS, D_MODEL, D_EXPERT), D_MODEL ** -0.5),
        "b_gate": nrm((DEPTH, N_EXPERTS, D_EXPERT), 0.02),
        "w_up": nrm((DEPTH, N_EXPERTS, D_MODEL, D_EXPERT), D_MODEL ** -0.5),
        "b_up": nrm((DEPTH, N_EXPERTS, D_EXPERT), 0.02),
        "w_down": nrm((DEPTH, N_EXPERTS, D_EXPERT, D_MODEL), D_EXPERT ** -0.5),
        "b_down": nrm((DEPTH, N_EXPERTS, D_MODEL), 0.02),
        "final_norm_g": 1.0 + nrm((D_MODEL,), 0.02),
    }


def reference(x_prompt, x_sample, cache_k, cache_v, page_table, state_conv, meta_tokens,
              norm1_g, w_in, lambda_q1, lambda_k1, lambda_q2, lambda_k2, subln_g,
              conv_w, conv_b, conv_ln_g, conv_ln_b, w_out, norm2_g, w_router, b_router,
              w_gate, b_gate, w_up, b_up, w_down, b_down, final_norm_g):
    dt = x_prompt.dtype
    B = x_prompt.shape[0]
    Bd, S = x_sample.shape[:2]
    xp = jnp.concatenate([jnp.broadcast_to(meta_tokens[None].astype(dt), (B, N_META, D_MODEL)),
                          x_prompt], axis=1)
    L = xp.shape[1]
    xs = x_sample
    past_len = page_table.shape[1] * PAGE_SIZE
    pos_p = jnp.arange(L)
    pos_s = past_len + jnp.arange(S)
    kp_l, vp_l, cp_l, ks_l, vs_l, cs_l = [], [], [], [], [], []
    for layer in range(DEPTH):
        lam = _diff_lambda(lambda_q1[layer], lambda_k1[layer], lambda_q2[layer], lambda_k2[layer], layer)
        moe_w = (w_router[layer], b_router[layer], w_gate[layer], b_gate[layer],
                 w_up[layer], b_up[layer], w_down[layer], b_down[layer])
        h = _rmsnorm(xp, norm1_g[layer])
        q, k, v, glu = _project(h, w_in[layer], pos_p)
        o_att = _attn_post(_attn_prompt(q, k, v, lam), subln_g[layer], layer)
        cbuf = jnp.concatenate([jnp.zeros((B, CONV_WIDTH - 1, D_CONV), glu.dtype), glu], axis=1)
        y_conv = _conv_branch(cbuf, conv_w[layer], conv_b[layer], conv_ln_g[layer], conv_ln_b[layer])
        xp = xp + jnp.concatenate([o_att.astype(dt), y_conv.astype(dt)], axis=-1) @ w_out[layer]
        xp = xp + _moe(_rmsnorm(xp, norm2_g[layer]).reshape(B * L, D_MODEL), *moe_w).reshape(B, L, D_MODEL)
        kp_l.append(k)
        vp_l.append(v)
        cp_l.append(cbuf[:, -(CONV_WIDTH - 1):])
        h = _rmsnorm(xs, norm1_g[layer])
        q, k, v, glu = _project(h, w_in[layer], pos_s)
        o_att = _attn_post(_attn_sample(q, k, v, cache_k, cache_v, page_table, layer, pos_s, lam),
                           subln_g[layer], layer)
        cbuf = jnp.concatenate([state_conv[layer].astype(glu.dtype), glu], axis=1)
        y_conv = _conv_branch(cbuf, conv_w[layer], conv_b[layer], conv_ln_g[layer], conv_ln_b[layer])
        xs = xs + jnp.concatenate([o_att.astype(dt), y_conv.astype(dt)], axis=-1) @ w_out[layer]
        xs = xs + _moe(_rmsnorm(xs, norm2_g[layer]).reshape(Bd * S, D_MODEL), *moe_w).reshape(Bd, S, D_MODEL)
        ks_l.append(k)
        vs_l.append(v)
        cs_l.append(cbuf[:, -(CONV_WIDTH - 1):])
    y_prompt = _rmsnorm(xp, final_norm_g)[:, N_META:]
    y_sample = _rmsnorm(xs, final_norm_g)
    k_prompt = jnp.stack(kp_l, 0)
    v_prompt = jnp.stack(vp_l, 0)
    conv_prompt = jnp.stack(cp_l, 0)
    k_sample = jnp.stack(ks_l, 0)
    v_sample = jnp.stack(vs_l, 0)
    conv_sample = jnp.stack(cs_l, 0)
    return (y_prompt, y_sample, k_prompt, v_prompt, conv_prompt, k_sample, v_sample, conv_sample)
```

```python
import functools
import math

import numpy as np
import jax
import jax.numpy as jnp
from jax import lax
from jax.experimental import pallas as pl
from jax.experimental.pallas import tpu as pltpu

F32 = jnp.float32
BF16 = jnp.bfloat16
U32 = jnp.uint32
I32 = jnp.int32

N_META = 16
N_HEADS = 4
D_HEAD = 64
D_VHEAD = 128
ROT_DIM = 16
ROPE_THETA = 500000.0
CONV_WIDTH = 31
N_EXPERTS = 32
TOP_K = 4
SWIGLU_LIMIT = 7.0
SWIGLU_ALPHA = 1.702
RMS_EPS = 1e-6
LN_EPS = 1e-5
PAGE_SIZE = 128

LANES = 128
ROW_TILE = 8
INVERT_UNROLL = 16
VMEM_LIMIT = 56 * 1024 * 1024

FRONT_TM = 512
ATTN_T = 512
DEC_PAGES = 16
CONV_T = 512
CONV_CHUNK = 64
CONV_HIST = 32
POST_TM = 512
MOE_TB = 256
COMB_T = 128


def _lambda_init(layer):
    return 0.8 - 0.6 * math.exp(-0.3 * layer)


def _cparams(sem):
    return pltpu.CompilerParams(dimension_semantics=sem, vmem_limit_bytes=VMEM_LIMIT)


def _rope_tables(pos):
    pos = np.asarray(pos, np.float64)
    inv = ROPE_THETA ** (-np.arange(0, ROT_DIM, 2, dtype=np.float64) / ROT_DIM)
    ang = pos[:, None] * inv[None, :]
    d = np.arange(LANES) % D_HEAD
    j = d % (ROT_DIM // 2)
    cos_l, sin_l = np.cos(ang)[:, j], np.sin(ang)[:, j]
    c = np.where(d < ROT_DIM, cos_l, 1.0)
    s1 = np.where(d < ROT_DIM // 2, -sin_l, 0.0)
    s2 = np.where((d >= ROT_DIM // 2) & (d < ROT_DIM), sin_l, 0.0)
    return (jnp.asarray(c, F32), jnp.asarray(s1, F32), jnp.asarray(s2, F32))


def _front_kernel(x_ref, g_ref, w_ref, c_ref, s1_ref, s2_ref,
                  q_ref, k_ref, kb_ref, v_ref, vb_ref, glu_ref, *, qk_w, v_w, c_w):
    x = x_ref[...]
    ms = jnp.mean(x * x, axis=-1, keepdims=True)
    h = ((x * lax.rsqrt(ms + RMS_EPS)) * g_ref[...]).astype(BF16)
    c, s1, s2 = c_ref[...], s1_ref[...], s2_ref[...]

    def proj(lo, width):
        return jnp.dot(h, w_ref[:, lo:lo + width], preferred_element_type=F32)

    def rope(z):
        outs = []
        for g in range(z.shape[1] // LANES):
            zg = z[:, g * LANES:(g + 1) * LANES]
            outs.append(zg * c + pltpu.roll(zg, LANES - ROT_DIM // 2, 1) * s1
                        + pltpu.roll(zg, ROT_DIM // 2, 1) * s2)
        return jnp.concatenate(outs, axis=1)

    q = rope(proj(0, qk_w))
    q_ref[...] = (q * (D_HEAD ** -0.5)).astype(BF16)
    k = rope(proj(qk_w, qk_w))
    k_ref[...] = k
    kb_ref[...] = k.astype(BF16)
    v = proj(2 * qk_w, v_w)
    v_ref[...] = v
    vb_ref[...] = v.astype(BF16)
    a = proj(2 * qk_w + v_w, c_w)
    gate = proj(2 * qk_w + v_w + c_w, c_w)
    glu_ref[...] = a * jax.nn.sigmoid(gate)


def _front(x, g, w_bf, tables, tm, table_blocks):
    rows, d = x.shape
    qk_w = N_HEADS * 2 * D_HEAD
    v_w = N_HEADS * D_VHEAD
    c_w = (w_bf.shape[1] - 2 * qk_w - v_w) // 2
    grid = rows // tm
    row_spec = lambda w: pl.BlockSpec((tm, w), lambda i: (i, 0))
    tab_spec = pl.BlockSpec((tm, LANES), lambda i: (i % table_blocks, 0))
    out_shape = (jax.ShapeDtypeStruct((rows, qk_w), BF16), jax.ShapeDtypeStruct((rows, qk_w), F32),
                 jax.ShapeDtypeStruct((rows, qk_w), BF16), jax.ShapeDtypeStruct((rows, v_w), F32),
                 jax.ShapeDtypeStruct((rows, v_w), BF16), jax.ShapeDtypeStruct((rows, c_w), F32))
    return pl.pallas_call(
        functools.partial(_front_kernel, qk_w=qk_w, v_w=v_w, c_w=c_w),
        grid=(grid,),
        in_specs=[row_spec(d), pl.BlockSpec((1, d), lambda i: (0, 0)),
                  pl.BlockSpec(w_bf.shape, lambda i: (0, 0)), tab_spec, tab_spec, tab_spec],
        out_specs=(row_spec(qk_w), row_spec(qk_w), row_spec(qk_w), row_spec(v_w), row_spec(v_w), row_spec(c_w)),
        out_shape=out_shape,
        compiler_params=_cparams(("parallel",)),
        name="front",
    )(x, g, w_bf, *tables)


def _attn_kernel(lam_ref, q_ref, k_ref, v_ref, km_ref, vm_ref, g_ref, o_ref, m_ref, l_ref, acc_ref,
                 *, t, out_scale):
    qi = pl.program_id(2)
    q = q_ref[...]
    lane = lax.broadcasted_iota(I32, q.shape, 1)
    zero = jnp.zeros_like(q)
    qc = (jnp.where(lane < D_HEAD, q, zero), jnp.where(lane >= D_HEAD, q, zero))
    nt = (((1,), (1,)), ((), ()))
    rep = t // LANES

    km, vm = km_ref[...], vm_ref[...]
    for c in range(2):
        s = lax.dot_general(qc[c], km, nt, preferred_element_type=F32)
        m = jnp.max(s, axis=1, keepdims=True)
        p = jnp.exp(s - m)
        m_ref[c] = jnp.broadcast_to(m, (t, LANES))
        l_ref[c] = jnp.broadcast_to(jnp.sum(p, axis=1, keepdims=True), (t, LANES))
        acc_ref[c] = jnp.dot(p.astype(BF16), vm, preferred_element_type=F32)

    def step(j, masked):
        start = pl.multiple_of(j * t, t)
        kblk = k_ref[pl.ds(start, t), :]
        vblk = v_ref[pl.ds(start, t), :]
        for c in range(2):
            s = lax.dot_general(qc[c], kblk, nt, preferred_element_type=F32)
            if masked:
                row = lax.broadcasted_iota(I32, s.shape, 0)
                col = lax.broadcasted_iota(I32, s.shape, 1)
                s = jnp.where(row >= col, s, -jnp.inf)
            m_prev = m_ref[c]
            m_new = jnp.maximum(m_prev, jnp.max(s, axis=1, keepdims=True))
            alpha = jnp.exp(m_prev - m_new)
            p = jnp.exp(s - jnp.concatenate([m_new] * rep, axis=1))
            l_ref[c] = alpha * l_ref[c] + jnp.sum(p, axis=1, keepdims=True)
            acc_ref[c] = alpha * acc_ref[c] + jnp.dot(p.astype(BF16), vblk, preferred_element_type=F32)
            m_ref[c] = m_new

    def body(j, carry):
        step(j, False)
        return carry

    lax.fori_loop(0, qi, body, 0)
    step(qi, True)

    o = acc_ref[0] / l_ref[0] - lam_ref[0, 0] * (acc_ref[1] / l_ref[1])
    ms = jnp.mean(o * o, axis=1, keepdims=True)
    o_ref[...] = ((o * lax.rsqrt(ms + RMS_EPS)) * g_ref[...] * out_scale).astype(BF16)


def _attn_prompt(lam, q, kb, vb, kmb, vmb, subln_g, batch, seq, out_scale):
    t = ATTN_T
    nq = seq // t
    blk = lambda b, h, i: (b * nq + i, h)
    return pl.pallas_call(
        functools.partial(_attn_kernel, t=t, out_scale=out_scale),
        grid=(batch, N_HEADS, nq),
        in_specs=[pl.BlockSpec(memory_space=pltpu.SMEM),
                  pl.BlockSpec((t, LANES), blk),
                  pl.BlockSpec((seq, LANES), lambda b, h, i: (b, h)),
                  pl.BlockSpec((seq, LANES), lambda b, h, i: (b, h)),
                  pl.BlockSpec((N_META, LANES), lambda b, h, i: (0, h)),
                  pl.BlockSpec((N_META, LANES), lambda b, h, i: (0, h)),
                  pl.BlockSpec((1, LANES), lambda b, h, i: (0, 0))],
        out_specs=pl.BlockSpec((t, LANES), blk),
        out_shape=jax.ShapeDtypeStruct(q.shape, BF16),
        scratch_shapes=[pltpu.VMEM((2, t, LANES), F32), pltpu.VMEM((2, t, LANES), F32),
                        pltpu.VMEM((2, t, LANES), F32)],
        compiler_params=_cparams(("parallel", "parallel", "arbitrary")),
        name="attn_prompt",
    )(lam, q, kb, vb, kmb, vmb, subln_g)


def _decode_kernel(pt_ref, lam_ref, qbd_ref, kn_ref, vn_ref, g_ref, k_hbm, v_hbm,
                   o_ref, m_ref, l_ref, acc_ref, kbuf, vbuf, sem, *, out_scale):
    b, j = pl.program_id(0), pl.program_id(1)
    n_b, n_j = pl.num_programs(0), pl.num_programs(1)
    n_pages = kbuf.shape[1]
    n_hc = qbd_ref.shape[0]
    width = qbd_ref.shape[1]
    rep = width // LANES
    step = b * n_j + j
    slot = step % 2

    def issue(bb, jj, s):
        for i in range(n_pages):
            page = pt_ref[bb, jj * n_pages + i]
            pltpu.make_async_copy(k_hbm.at[page], kbuf.at[s, i], sem.at[0, s]).start()
            pltpu.make_async_copy(v_hbm.at[page], vbuf.at[s, i], sem.at[1, s]).start()

    @pl.when(step == 0)
    def _():
        issue(0, 0, 0)

    @pl.when(step + 1 < n_b * n_j)
    def _():
        wrap = j + 1 == n_j
        issue(jnp.where(wrap, b + 1, b), jnp.where(wrap, 0, j + 1), 1 - slot)

    pltpu.make_async_copy(k_hbm.at[pl.ds(0, n_pages)], kbuf.at[slot], sem.at[0, slot]).wait()
    pltpu.make_async_copy(v_hbm.at[pl.ds(0, n_pages)], vbuf.at[slot], sem.at[1, slot]).wait()

    qf = qbd_ref[...]
    qb = qf.astype(BF16)

    @pl.when(j == 0)
    def _():
        m_ref[...] = jnp.full(m_ref.shape, -jnp.inf, F32)
        l_ref[...] = jnp.zeros(l_ref.shape, F32)
        acc_ref[...] = jnp.zeros(acc_ref.shape, F32)

    ss = [jnp.dot(qb, kbuf[slot, i].astype(BF16), preferred_element_type=F32)
          for i in range(n_pages)]
    m_cur = ss[0].max(axis=1, keepdims=True)
    for s in ss[1:]:
        m_cur = jnp.maximum(m_cur, s.max(axis=1, keepdims=True))
    m_prev = m_ref[...]
    m_new = jnp.maximum(m_prev, m_cur)
    alpha = jnp.exp(m_prev - m_new)
    lsum = jnp.zeros((n_hc, 1), F32)
    pv = jnp.zeros((n_hc, width), F32)
    for i in range(n_pages):
        p = jnp.exp(ss[i] - m_new)
        lsum = lsum + jnp.sum(p, axis=1, keepdims=True)
        pb = p.astype(BF16)
        pv = pv + jnp.concatenate(
            [jnp.dot(pb, vbuf[slot, i, pl.ds(h, PAGE_SIZE, stride=N_HEADS), :].astype(BF16),
                     preferred_element_type=F32) for h in range(N_HEADS)], axis=1)
    l_new = alpha * l_ref[...] + lsum
    acc_new = jnp.concatenate([alpha] * rep, axis=1) * acc_ref[...] + pv
    m_ref[...] = m_new
    l_ref[...] = l_new
    acc_ref[...] = acc_new

    @pl.when(j == pl.num_programs(1) - 1)
    def _():
        s_new = jnp.sum(qf * kn_ref[...], axis=1, keepdims=True)
        m_f = jnp.maximum(m_new, s_new)
        a2 = jnp.exp(m_new - m_f)
        p_new = jnp.exp(s_new - m_f)
        l_f = a2 * l_new + p_new
        acc_f = jnp.concatenate([a2] * rep, axis=1) * acc_new + jnp.concatenate([p_new] * rep, axis=1) * vn_ref[...]
        o = acc_f / jnp.concatenate([l_f] * rep, axis=1)
        row = lax.broadcasted_iota(I32, o.shape, 0)
        ln = lax.broadcasted_iota(I32, o.shape, 1)
        own = (ln // D_VHEAD) == (row // 2)
        coef = jnp.where(row % 2 == 0, 1.0, -lam_ref[0, 0])
        od = jnp.sum(jnp.where(own, o * coef, 0.0), axis=0, keepdims=True)
        parts = []
        for h in range(rep):
            oh = od[:, h * D_VHEAD:(h + 1) * D_VHEAD]
            ms = jnp.mean(oh * oh, axis=1, keepdims=True)
            parts.append(oh * lax.rsqrt(ms + RMS_EPS))
        o_ref[...] = jnp.concatenate(parts, axis=1) * g_ref[...] * out_scale


def _attn_decode(page_table, lam, qbd, k_new, v_new, g_row, cache_k, cache_v, out_scale):
    bd, n_hc, width = qbd.shape
    n_steps = page_table.shape[1] // DEC_PAGES
    row_spec = pl.BlockSpec((None, 1, width), lambda b, j, pt: (b, 0, 0))
    grid_spec = pltpu.PrefetchScalarGridSpec(
        num_scalar_prefetch=1,
        grid=(bd, n_steps),
        in_specs=[pl.BlockSpec(memory_space=pltpu.SMEM),
                  pl.BlockSpec((None, n_hc, width), lambda b, j, pt: (b, 0, 0)),
                  row_spec, row_spec,
                  pl.BlockSpec((1, width), lambda b, j, pt: (0, 0)),
                  pl.BlockSpec(memory_space=pl.ANY), pl.BlockSpec(memory_space=pl.ANY)],
        out_specs=row_spec,
        scratch_shapes=[pltpu.VMEM((n_hc, LANES), F32), pltpu.VMEM((n_hc, LANES), F32),
                        pltpu.VMEM((n_hc, width), F32),
                        pltpu.VMEM((2, DEC_PAGES) + cache_k.shape[1:], F32),
                        pltpu.VMEM((2, DEC_PAGES) + cache_v.shape[1:], F32),
                        pltpu.SemaphoreType.DMA((2, 2))],
    )
    return pl.pallas_call(
        functools.partial(_decode_kernel, out_scale=out_scale),
        grid_spec=grid_spec,
        out_shape=jax.ShapeDtypeStruct((bd, 1, width), F32),
        compiler_params=_cparams(("arbitrary", "arbitrary")),
        name="attn_decode",
    )(page_table, lam, qbd, k_new, v_new, g_row, cache_k, cache_v)


def _ln_swish(y, g, b):
    mu = jnp.mean(y, axis=-1, keepdims=True)
    var = jnp.mean(jnp.square(y - mu), axis=-1, keepdims=True)
    z = (y - mu) * lax.rsqrt(var + LN_EPS) * g + b
    return z * jax.nn.sigmoid(z)


def _conv_prompt_kernel(cur_ref, prev_ref, hist_ref, w_ref, b_ref, g_ref, beta_ref, o_ref, buf_ref):
    i = pl.program_id(1)
    t = cur_ref.shape[0]
    buf_ref[0:CONV_HIST, :] = jnp.where(i == 0, hist_ref[...], prev_ref[...])
    buf_ref[CONV_HIST:CONV_HIST + t, :] = cur_ref[...]
    off = CONV_HIST - (CONV_WIDTH - 1)

    def chunk(r, carry):
        r0 = pl.multiple_of(r * CONV_CHUNK, CONV_CHUNK)
        win = buf_ref[pl.ds(r0, CONV_CHUNK + CONV_HIST), :]
        acc = jnp.zeros((CONV_CHUNK, cur_ref.shape[1]), F32)
        for s in range(8):
            taps = [w for w in range(CONV_WIDTH) if (off + w) % 8 == s]
            ws = win if s == 0 else win[s:s + CONV_CHUNK + CONV_HIST - 8, :]
            for w in taps:
                base = off + w - s
                acc = acc + ws[base:base + CONV_CHUNK, :] * w_ref[w:w + 1, :]
        y = acc + b_ref[...]
        o_ref[pl.ds(r0, CONV_CHUNK), :] = _ln_swish(y, g_ref[...], beta_ref[...]).astype(o_ref.dtype)
        return carry

    lax.fori_loop(0, t // CONV_CHUNK, chunk, 0)


def _conv_prompt(glu, hist, w, b, g, beta):
    batch, seq, c = glu.shape
    t = CONV_T
    per = t // CONV_HIST
    vec = pl.BlockSpec((1, c), lambda bb, i: (0, 0))
    return pl.pallas_call(
        _conv_prompt_kernel,
        grid=(batch, seq // t),
        in_specs=[pl.BlockSpec((None, t, c), lambda bb, i: (bb, i, 0)),
                  pl.BlockSpec((None, CONV_HIST, c), lambda bb, i: (bb, jnp.maximum(i * per - 1, 0), 0)),
                  pl.BlockSpec((CONV_HIST, c), lambda bb, i: (0, 0)),
                  pl.BlockSpec(w.shape, lambda bb, i: (0, 0)), vec, vec, vec],
        out_specs=pl.BlockSpec((None, t, c), lambda bb, i: (bb, i, 0)),
        out_shape=jax.ShapeDtypeStruct(glu.shape, BF16),
        scratch_shapes=[pltpu.VMEM((CONV_HIST + t, c), F32)],
        compiler_params=_cparams(("parallel", "parallel")),
        name="conv_prompt",
    )(glu, glu, hist, w, b, g, beta)


def _conv_sample_kernel(st_ref, glu_ref, w_ref, b_ref, g_ref, beta_ref, o_ref):
    n_hist = st_ref.shape[0]
    y = glu_ref[...] * w_ref[n_hist:n_hist + 1, :] + b_ref[...]
    for w in range(n_hist):
        y = y + st_ref[w] * w_ref[w:w + 1, :]
    o_ref[...] = _ln_swish(y, g_ref[...], beta_ref[...]).astype(o_ref.dtype)


def _conv_sample(state, glu, w, b, g, beta):
    n_hist, bd, c = state.shape
    tb = 16
    vec = pl.BlockSpec((1, c), lambda i: (0, 0))
    return pl.pallas_call(
        _conv_sample_kernel,
        grid=(bd // tb,),
        in_specs=[pl.BlockSpec((n_hist, tb, c), lambda i: (0, i, 0)),
                  pl.BlockSpec((tb, c), lambda i: (i, 0)),
                  pl.BlockSpec(w.shape, lambda i: (0, 0)), vec, vec, vec],
        out_specs=pl.BlockSpec((tb, c), lambda i: (i, 0)),
        out_shape=jax.ShapeDtypeStruct((bd, c), BF16),
        compiler_params=_cparams(("parallel",)),
        name="conv_sample",
    )(state, glu, w, b, g, beta)


def _post_rows(n, oa, yc, x, wo_ref, g2_ref, wrh_ref, wrl_ref, br_ref, tri_ref,
               x1_ref, ht_ref, route_ref, run_ref):
    half = oa.shape[1]
    x1 = (x + jnp.dot(oa, wo_ref[0:half, :], preferred_element_type=F32)
          + jnp.dot(yc, wo_ref[half:2 * half, :], preferred_element_type=F32))
    x1_ref[0:n, :] = x1
    ms = jnp.mean(x1 * x1, axis=-1, keepdims=True)
    h = (x1 * lax.rsqrt(ms + RMS_EPS)) * g2_ref[...]
    for s in range(ROW_TILE):
        ht_ref[pl.ds(s, n, stride=ROW_TILE), :] = h[:, s * LANES:(s + 1) * LANES]

    h_hi = h.astype(BF16)
    h_lo = (h - h_hi.astype(F32)).astype(BF16)
    logits = (jnp.dot(h_hi, wrh_ref[...], preferred_element_type=F32)
              + jnp.dot(h_lo, wrh_ref[...], preferred_element_type=F32)
              + jnp.dot(h_hi, wrl_ref[...], preferred_element_type=F32)) + br_ref[...]
    lane = lax.broadcasted_iota(I32, logits.shape, 1)
    work = jnp.where(lane < N_EXPERTS, logits, -jnp.inf)
    vals, idxs = [], []
    onehot = jnp.zeros(logits.shape, F32)
    for _ in range(TOP_K):
        m = jnp.max(work, axis=1, keepdims=True)
        idx = jnp.min(jnp.where(work == m, lane, LANES), axis=1, keepdims=True)
        sel = lane == idx
        vals.append(m)
        idxs.append(idx)
        onehot = jnp.where(sel, 1.0, onehot)
        work = jnp.where(sel, -jnp.inf, work)
    exps = [jnp.exp(v - vals[0]) for v in vals]
    denom = exps[0] + exps[1] + exps[2] + exps[3]
    gates = [e / denom for e in exps]

    before = jnp.dot(tri_ref[0:n, 0:n], onehot.astype(BF16), preferred_element_type=F32) + run_ref[...]
    route = jnp.zeros(logits.shape, F32)
    for k in range(TOP_K):
        rank = jnp.sum(jnp.where(lane == idxs[k], before, 0.0), axis=1, keepdims=True)
        route = jnp.where(lane == k, idxs[k].astype(F32), route)
        route = jnp.where(lane == TOP_K + k, rank, route)
        route = jnp.where(lane == 2 * TOP_K + k, gates[k], route)
    route_ref[0:n, :] = route
    run_ref[...] = run_ref[...] + jnp.sum(onehot, axis=0, keepdims=True)


def _post_kernel(oap_ref, ycp_ref, xp_ref, oas_ref, ycs_ref, xs_ref,
                 wo_ref, g2_ref, wrh_ref, wrl_ref, br_ref, tri_ref,
                 x1_ref, ht_ref, route_ref, cnt_ref, run_ref):
    i = pl.program_id(0)
    last = pl.num_programs(0) - 1
    tm = xp_ref.shape[0]
    ns = xs_ref.shape[0]
    shared = (wo_ref, g2_ref, wrh_ref, wrl_ref, br_ref, tri_ref, x1_ref, ht_ref, route_ref, run_ref)

    @pl.when(i == 0)
    def _():
        run_ref[...] = jnp.zeros(run_ref.shape, F32)

    @pl.when(i < last)
    def _():
        _post_rows(tm, oap_ref[...], ycp_ref[...], xp_ref[...], *shared)

    @pl.when(i == last)
    def _():
        _post_rows(ns, oas_ref[...], ycs_ref[...], xs_ref[...], *shared)
        x1_ref[ns:tm, :] = jnp.zeros((tm - ns, x1_ref.shape[1]), F32)
        route_ref[ns:tm, :] = jnp.zeros((tm - ns, LANES), F32)
        ht_ref[ns * ROW_TILE:tm * ROW_TILE, :] = jnp.zeros(((tm - ns) * ROW_TILE, LANES), F32)
        cnt_ref[...] = run_ref[...]


def _post(oa_p, yc_p, x_p, oa_s, yc_s, x_s, wo_bf, g2, wr_hi, wr_lo, br):
    rows, d = x_p.shape
    ns = x_s.shape[0]
    tm = POST_TM
    n_steps = rows // tm + 1
    n_pad = n_steps * tm
    half = oa_p.shape[1]
    tri = (np.arange(tm)[:, None] > np.arange(tm)[None, :]).astype(np.float32)
    tri = jnp.asarray(tri, BF16)
    last_p = rows // tm - 1
    p_spec = lambda w: pl.BlockSpec((tm, w), lambda i: (jnp.minimum(i, last_p), 0))
    full = lambda a: pl.BlockSpec(a.shape, lambda i: (0, 0))
    out_spec = lambda r, w: pl.BlockSpec((r, w), lambda i: (i, 0))
    return pl.pallas_call(
        _post_kernel,
        grid=(n_steps,),
        in_specs=[p_spec(half), p_spec(half), p_spec(d), full(oa_s), full(yc_s), full(x_s),
                  full(wo_bf), full(g2), full(wr_hi), full(wr_lo), full(br), full(tri)],
        out_specs=(out_spec(tm, d), out_spec(tm * ROW_TILE, LANES), out_spec(tm, LANES),
                   pl.BlockSpec((1, LANES), lambda i: (0, 0))),
        out_shape=(jax.ShapeDtypeStruct((n_pad, d), F32), jax.ShapeDtypeStruct((n_pad * ROW_TILE, LANES), F32),
                   jax.ShapeDtypeStruct((n_pad, LANES), F32), jax.ShapeDtypeStruct((1, LANES), F32)),
        scratch_shapes=[pltpu.VMEM((1, LANES), F32)],
        compiler_params=_cparams(("arbitrary",)),
        name="post",
    )(oa_p, yc_p, x_p, oa_s, yc_s, x_s, wo_bf, g2, wr_hi, wr_lo, br, tri)


def _moe_kernel(be_ref, nv_ref, tok_cur_ref, tok_nxt_ref, h_hbm,
                wg_ref, wu_ref, wd_ref, bg_ref, bu_ref, bd_ref, ys_ref,
                xbuf, sem, wgb, wub, wdb):
    i = pl.program_id(0)
    nv = nv_ref[0]
    tb = xbuf.shape[1] // ROW_TILE
    slot = i % 2

    def issue(row_ref, s):
        for r in range(tb):
            src = h_hbm.at[pl.ds(pl.multiple_of(row_ref[r], ROW_TILE), ROW_TILE)]
            pltpu.make_async_copy(src, xbuf.at[s, pl.ds(r * ROW_TILE, ROW_TILE)], sem.at[s]).start()

    def wait(s):
        pltpu.make_async_copy(h_hbm.at[pl.ds(0, tb * ROW_TILE)], xbuf.at[s], sem.at[s]).wait()

    @pl.when(i == 0)
    def _():
        issue(tok_cur_ref, 0)

    @pl.when(i < nv)
    def _():
        changed = jnp.logical_or(i == 0, be_ref[i] != be_ref[jnp.maximum(i - 1, 0)])

        @pl.when(changed)
        def _():
            wgb[...] = wg_ref[...].astype(BF16)
            wub[...] = wu_ref[...].astype(BF16)
            wdb[...] = wd_ref[...].astype(BF16)

        wait(slot)
        x = jnp.concatenate([xbuf[slot, pl.ds(s, tb, stride=ROW_TILE), :] for s in range(ROW_TILE)],
                            axis=1).astype(BF16)
        issue(tok_nxt_ref, 1 - slot)

        g = jnp.minimum(jnp.dot(x, wgb[...], preferred_element_type=F32) + bg_ref[...], SWIGLU_LIMIT)
        u = jnp.clip(jnp.dot(x, wub[...], preferred_element_type=F32) + bu_ref[...], -SWIGLU_LIMIT, SWIGLU_LIMIT)
        a = (g * jax.nn.sigmoid(SWIGLU_ALPHA * g) * (u + 1.0)).astype(BF16)
        y = jnp.dot(a, wdb[...], preferred_element_type=F32) + bd_ref[...]
        for s in range(ROW_TILE):
            ys_ref[pl.ds(s, tb, stride=ROW_TILE), :] = y[:, s * LANES:(s + 1) * LANES]

    @pl.when(i == nv)
    def _():
        wait(slot)

    @pl.when(i >= nv)
    def _():
        ys_ref[...] = jnp.zeros(ys_ref.shape, F32)


def _moe(block_e, n_valid, slot_row, h_tiles, wg, wu, wd, bg, bu, bd):
    p = slot_row.shape[0]
    tb = MOE_TB
    nb = p // tb
    n_e, d, de = wg.shape
    assert d == ROW_TILE * LANES
    w_spec = lambda r, c: pl.BlockSpec((None, r, c), lambda i, be, nv: (be[i], 0, 0))
    b_spec = lambda c: pl.BlockSpec((None, 1, c), lambda i, be, nv: (be[i], 0, 0))
    last = lambda i, nv: jnp.minimum(i, nv[0] - 1)
    grid_spec = pltpu.PrefetchScalarGridSpec(
        num_scalar_prefetch=2,
        grid=(nb,),
        in_specs=[pl.BlockSpec((tb,), lambda i, be, nv: (last(i, nv),), memory_space=pltpu.SMEM),
                  pl.BlockSpec((tb,), lambda i, be, nv: (last(i + 1, nv),), memory_space=pltpu.SMEM),
                  pl.BlockSpec(memory_space=pl.ANY),
                  w_spec(d, de), w_spec(d, de), w_spec(de, d), b_spec(de), b_spec(de), b_spec(d)],
        out_specs=pl.BlockSpec((tb * ROW_TILE, LANES), lambda i, be, nv: (i, 0)),
        scratch_shapes=[pltpu.VMEM((2, tb * ROW_TILE, LANES), F32), pltpu.SemaphoreType.DMA((2,)),
                        pltpu.VMEM((d, de), BF16), pltpu.VMEM((d, de), BF16), pltpu.VMEM((de, d), BF16)],
    )
    return pl.pallas_call(
        _moe_kernel,
        grid_spec=grid_spec,
        out_shape=jax.ShapeDtypeStruct((p * ROW_TILE, LANES), F32),
        compiler_params=_cparams(("arbitrary",)),
        name="moe",
    )(block_e, n_valid, slot_row, slot_row, h_tiles, wg, wu, wd,
      bg.reshape(n_e, 1, de), bu.reshape(n_e, 1, de), bd.reshape(n_e, 1, d))


def _combine_kernel(pos_cur_ref, pos_nxt_ref, ys_hbm, x1_ref, route_ref, fg_ref, yp_ref, ys_ref, buf, sem):
    i = pl.program_id(0)
    n = pl.num_programs(0)
    t = x1_ref.shape[0]
    slot = i % 2

    def issue(pos_ref, s):
        for r in range(t):
            for k in range(TOP_K):
                src = ys_hbm.at[pl.ds(pl.multiple_of(pos_ref[r * TOP_K + k], ROW_TILE), ROW_TILE)]
                pltpu.make_async_copy(src, buf.at[s, k, pl.ds(r * ROW_TILE, ROW_TILE)], sem.at[s]).start()

    def wait(s):
        for k in range(TOP_K):
            pltpu.make_async_copy(ys_hbm.at[pl.ds(0, t * ROW_TILE)], buf.at[s, k], sem.at[s]).wait()

    @pl.when(i == 0)
    def _():
        issue(pos_cur_ref, 0)

    wait(slot)
    issue(pos_nxt_ref, 1 - slot)
    route = route_ref[...]
    gates = [route[:, 2 * TOP_K + k:2 * TOP_K + k + 1] for k in range(TOP_K)]
    parts = []
    for s in range(ROW_TILE):
        acc = x1_ref[:, s * LANES:(s + 1) * LANES]
        for k in range(TOP_K):
            acc = acc + gates[k] * buf[slot, k, pl.ds(s, t, stride=ROW_TILE), :]
        parts.append(acc)
    y = jnp.concatenate(parts, axis=1)
    ms = jnp.mean(y * y, axis=-1, keepdims=True)
    out = (y * lax.rsqrt(ms + RMS_EPS)) * fg_ref[...]

    @pl.when(i < n - 1)
    def _():
        yp_ref[...] = out

    @pl.when(i == n - 1)
    def _():
        ys_ref[...] = out
        wait(1 - slot)


def _combine(pos_row, ys_tiles, x1, route, fg, n_prompt, n_sample):
    d = x1.shape[1]
    t = COMB_T
    assert n_sample == t and n_prompt % t == 0
    n = n_prompt // t + 1
    return pl.pallas_call(
        _combine_kernel,
        grid=(n,),
        in_specs=[pl.BlockSpec((t * TOP_K,), lambda i: (i,), memory_space=pltpu.SMEM),
                  pl.BlockSpec((t * TOP_K,), lambda i: (jnp.minimum(i + 1, n - 1),), memory_space=pltpu.SMEM),
                  pl.BlockSpec(memory_space=pl.ANY),
                  pl.BlockSpec((t, d), lambda i: (i, 0)),
                  pl.BlockSpec((t, LANES), lambda i: (i, 0)),
                  pl.BlockSpec((1, d), lambda i: (0, 0))],
        out_specs=(pl.BlockSpec((t, d), lambda i: (jnp.minimum(i, n - 2), 0)),
                   pl.BlockSpec((t, d), lambda i: (0, 0))),
        out_shape=(jax.ShapeDtypeStruct((n_prompt, d), F32), jax.ShapeDtypeStruct((n_sample, d), F32)),
        scratch_shapes=[pltpu.VMEM((2, TOP_K, t * ROW_TILE, LANES), F32), pltpu.SemaphoreType.DMA((2,))],
        compiler_params=_cparams(("arbitrary",)),
        name="combine",
    )(pos_row, pos_row, ys_tiles, x1, route, fg)


def _invert_kernel(pos_ref, init_ref, out_ref):
    pltpu.sync_copy(init_ref, out_ref)
    n = pos_ref.shape[0]

    def body(j, carry):
        base = j * INVERT_UNROLL
        row0 = base * (ROW_TILE // TOP_K)
        for u in range(INVERT_UNROLL):
            out_ref[pos_ref[base + u]] = row0 + (u // TOP_K) * ROW_TILE
        return carry

    lax.fori_loop(0, n // INVERT_UNROLL, body, 0)


def _invert(pos_flat, fill_row, n_slots):
    init = jnp.full((n_slots,), fill_row, I32)
    smem = pl.BlockSpec(memory_space=pltpu.SMEM)
    return pl.pallas_call(
        _invert_kernel,
        in_specs=[smem, smem],
        out_specs=smem,
        out_shape=jax.ShapeDtypeStruct((n_slots,), I32),
        name="invert",
    )(pos_flat, init)


def _routing_tables(route, counts, n_tok):
    tb = MOE_TB
    a = n_tok * TOP_K
    idx = route[:, 0:TOP_K].astype(I32)
    rank = route[:, TOP_K:2 * TOP_K].astype(I32)
    cnt = counts[0, :N_EXPERTS].astype(I32)
    padded = ((cnt + tb - 1) // tb) * tb
    ends = jnp.cumsum(padded)
    pad_start = ends - padded
    onehot = idx[..., None] == jnp.arange(N_EXPERTS, dtype=I32)
    pos = jnp.sum(jnp.where(onehot, pad_start, 0), axis=-1) + rank
    nb = a // tb + N_EXPERTS
    pos_flat = pos.reshape(a)
    slot_row = _invert(pos_flat, n_tok * ROW_TILE, nb * tb)
    block_start = jnp.arange(nb, dtype=I32) * tb
    n_valid = (ends[-1] // tb).astype(I32)
    be = jnp.sum(block_start[:, None] >= ends[None, :], axis=1).astype(I32)
    be_last = jnp.sum(jnp.where(jnp.arange(nb) == n_valid - 1, be, 0))
    block_e = jnp.where(jnp.arange(nb) < n_valid, jnp.minimum(be, N_EXPERTS - 1), be_last).astype(I32)
    return pos_flat * ROW_TILE, slot_row, block_e, n_valid.reshape(1)


def kernel(x_prompt, x_sample, cache_k, cache_v, page_table, state_conv, meta_tokens, norm1_g, w_in,
           lambda_q1, lambda_k1, lambda_q2, lambda_k2, subln_g, conv_w, conv_b, conv_ln_g, conv_ln_b,
           w_out, norm2_g, w_router, b_router, w_gate, b_gate, w_up, b_up, w_down, b_down, final_norm_g):
    depth = norm1_g.shape[0]
    assert depth == 1, "single trunk layer"
    layer = 0
    batch, seq, d = x_prompt.shape
    bd, dec_seq, _ = x_sample.shape
    assert dec_seq == 1
    past_len = page_table.shape[1] * PAGE_SIZE
    qk_w = N_HEADS * 2 * D_HEAD
    v_w = N_HEADS * D_VHEAD
    c_conv = conv_w.shape[2]
    out_scale = 1.0 - _lambda_init(layer)

    lam = (jnp.exp(jnp.sum(lambda_q1[layer] * lambda_k1[layer]))
           - jnp.exp(jnp.sum(lambda_q2[layer] * lambda_k2[layer])) + _lambda_init(layer)).reshape(1, 1).astype(F32)
    w_in_bf = w_in[layer].astype(BF16)
    g1 = norm1_g[layer].reshape(1, d)

    tab_p = _rope_tables(np.arange(N_META, N_META + seq))
    q_p, k_p, kb_p, v_p, vb_p, glu_p = _front(x_prompt.reshape(batch * seq, d), g1, w_in_bf, tab_p,
                                              FRONT_TM, seq // FRONT_TM)
    x_sm = jnp.concatenate([x_sample.reshape(bd, d), meta_tokens.astype(F32)], axis=0)
    tab_sm = _rope_tables(np.concatenate([np.full(bd, past_len), np.arange(N_META)]))
    q_sm, k_sm, kb_sm, v_sm, vb_sm, glu_sm = _front(x_sm, g1, w_in_bf, tab_sm, bd + N_META, 1)

    sub_g = subln_g[layer].reshape(1, D_VHEAD)
    oa_p = _attn_prompt(lam, q_p, kb_p, vb_p, kb_sm[bd:], vb_sm[bd:], sub_g, batch, seq, out_scale)

    n_hc = 2 * N_HEADS
    hc_mask = (jnp.arange(qk_w)[None, :] // D_HEAD) == jnp.arange(n_hc)[:, None]
    qbd = jnp.where(hc_mask[None], q_sm[:bd].astype(F32)[:, None, :], 0.0)
    n_phys = cache_k.shape[1]
    kt = jnp.transpose(cache_k[layer], (0, 2, 3, 4, 1)).reshape(n_phys, qk_w, PAGE_SIZE)
    vr = cache_v[layer].reshape(n_phys, PAGE_SIZE * N_HEADS, D_VHEAD)
    oa_s = _attn_decode(page_table, lam, qbd, k_sm[:bd].reshape(bd, 1, qk_w), v_sm[:bd].reshape(bd, 1, v_w),
                        jnp.tile(sub_g, (1, N_HEADS)), kt, vr, out_scale)

    cw = jnp.concatenate([conv_w[layer], jnp.zeros((CONV_HIST - CONV_WIDTH, c_conv), F32)], axis=0)
    cb, cg, cbeta = conv_b[layer].reshape(1, -1), conv_ln_g[layer].reshape(1, -1), conv_ln_b[layer].reshape(1, -1)
    hist = jnp.concatenate([jnp.zeros((CONV_HIST - N_META, c_conv), F32), glu_sm[bd:]], axis=0)
    yc_p = _conv_prompt(glu_p.reshape(batch, seq, c_conv), hist, cw, cb, cg, cbeta)
    yc_s = _conv_sample(jnp.transpose(state_conv[layer], (1, 0, 2)), glu_sm[:bd], cw, cb, cg, cbeta)

    n_prompt = batch * seq
    n_tok = n_prompt + bd
    wr = jnp.zeros((d, LANES), F32).at[:, :N_EXPERTS].set(w_router[layer])
    wr_hi = wr.astype(BF16)
    wr_lo = (wr - wr_hi.astype(F32)).astype(BF16)
    br = jnp.zeros((1, LANES), F32).at[0, :N_EXPERTS].set(b_router[layer])
    x1, h_tiles, route, counts = _post(oa_p, yc_p.reshape(n_prompt, c_conv), x_prompt.reshape(n_prompt, d),
                                       oa_s.reshape(bd, v_w).astype(BF16), yc_s, x_sample.reshape(bd, d),
                                       w_out[layer].astype(BF16), norm2_g[layer].reshape(1, d), wr_hi, wr_lo, br)

    pos_row, slot_row, block_e, n_valid = _routing_tables(route[:n_tok], counts, n_tok)
    ys_tiles = _moe(block_e, n_valid, slot_row, h_tiles, w_gate[layer], w_up[layer], w_down[layer],
                    b_gate[layer], b_up[layer], b_down[layer])
    y_p, y_s = _combine(pos_row, ys_tiles, x1, route, final_norm_g.reshape(1, d), n_prompt, bd)

    y_prompt = y_p.reshape(batch, seq, d)
    y_sample = y_s.reshape(bd, 1, d)
    k_meta = jnp.broadcast_to(k_sm[bd:][None], (batch, N_META, qk_w))
    v_meta = jnp.broadcast_to(v_sm[bd:][None], (batch, N_META, v_w))
    k_prompt = jnp.concatenate([k_meta, k_p.reshape(batch, seq, qk_w)], axis=1)
    v_prompt = jnp.concatenate([v_meta, v_p.reshape(batch, seq, v_w)], axis=1)
    k_prompt = k_prompt.reshape(1, batch, seq + N_META, N_HEADS, 2, D_HEAD)
    v_prompt = v_prompt.reshape(1, batch, seq + N_META, N_HEADS, D_VHEAD)
    n_keep = CONV_WIDTH - 1
    conv_prompt = glu_p.reshape(batch, seq, c_conv)[:, seq - n_keep:][None]
    k_sample = k_sm[:bd].reshape(1, bd, 1, N_HEADS, 2, D_HEAD)
    v_sample = v_sm[:bd].reshape(1, bd, 1, N_HEADS, D_VHEAD)
    conv_sample = jnp.concatenate([state_conv[layer][:, 1:], glu_sm[:bd][:, None, :]], axis=1)[None]
    return (y_prompt, y_sample, k_prompt, v_prompt, conv_prompt, k_sample, v_sample, conv_sample)
```

```python
import functools
import math

import numpy as np
import jax
import jax.numpy as jnp
from jax import lax
from jax.experimental import pallas as pl
from jax.experimental.pallas import tpu as pltpu

F32 = jnp.float32
BF16 = jnp.bfloat16
I32 = jnp.int32

N_META = 16
N_HEADS = 4
D_HEAD = 64
D_VHEAD = 128
ROT_DIM = 16
ROPE_THETA = 500000.0
CONV_WIDTH = 31
N_EXPERTS = 32
TOP_K = 4
SWIGLU_LIMIT = 7.0
SWIGLU_ALPHA = 1.702
RMS_EPS = 1e-6
LN_EPS = 1e-5
PAGE_SIZE = 128

LANES = 128
ROW_TILE = 8
INVERT_UNROLL = 16
N_DMA_QUEUES = 2
VMEM_LIMIT = 56 * 1024 * 1024

FRONT_TM = 512
ATTN_T = 512
DEC_PAGES = 16
CONV_T = 512
CONV_CHUNK = 64
CONV_HIST = 32
POST_TM = 512
MOE_TB = 256
COMB_T = 128


def _lambda_init(layer):
    return 0.8 - 0.6 * math.exp(-0.3 * layer)


def _cparams(sem):
    return pltpu.CompilerParams(dimension_semantics=sem, vmem_limit_bytes=VMEM_LIMIT)


def _rope_tables(pos):
    pos = np.asarray(pos, np.float64)
    inv = ROPE_THETA ** (-np.arange(0, ROT_DIM, 2, dtype=np.float64) / ROT_DIM)
    ang = pos[:, None] * inv[None, :]
    d = np.arange(LANES) % D_HEAD
    j = d % (ROT_DIM // 2)
    cos_l, sin_l = np.cos(ang)[:, j], np.sin(ang)[:, j]
    c = np.where(d < ROT_DIM, cos_l, 1.0)
    s1 = np.where(d < ROT_DIM // 2, -sin_l, 0.0)
    s2 = np.where((d >= ROT_DIM // 2) & (d < ROT_DIM), sin_l, 0.0)
    return (jnp.asarray(c, F32), jnp.asarray(s1, F32), jnp.asarray(s2, F32))


def _front_kernel(x_ref, g_ref, w_ref, c_ref, s1_ref, s2_ref,
                  q_ref, k_ref, kb_ref, v_ref, vb_ref, glu_ref, *, qk_w, v_w, c_w):
    x = x_ref[...]
    ms = jnp.mean(x * x, axis=-1, keepdims=True)
    h = ((x * lax.rsqrt(ms + RMS_EPS)) * g_ref[...]).astype(BF16)
    c, s1, s2 = c_ref[...], s1_ref[...], s2_ref[...]

    def proj(lo, width):
        return jnp.dot(h, w_ref[:, lo:lo + width], preferred_element_type=F32)

    def rope(z):
        outs = []
        for g in range(z.shape[1] // LANES):
            zg = z[:, g * LANES:(g + 1) * LANES]
            outs.append(zg * c + pltpu.roll(zg, LANES - ROT_DIM // 2, 1) * s1
                        + pltpu.roll(zg, ROT_DIM // 2, 1) * s2)
        return jnp.concatenate(outs, axis=1)

    q = rope(proj(0, qk_w))
    q_ref[...] = (q * (D_HEAD ** -0.5)).astype(BF16)
    k = rope(proj(qk_w, qk_w))
    k_ref[...] = k
    kb_ref[...] = k.astype(BF16)
    v = proj(2 * qk_w, v_w)
    v_ref[...] = v
    vb_ref[...] = v.astype(BF16)
    a = proj(2 * qk_w + v_w, c_w)
    gate = proj(2 * qk_w + v_w + c_w, c_w)
    glu_ref[...] = a * jax.nn.sigmoid(gate)


def _front(x, g, w_bf, tables, tm, table_blocks):
    rows, d = x.shape
    qk_w = N_HEADS * 2 * D_HEAD
    v_w = N_HEADS * D_VHEAD
    c_w = (w_bf.shape[1] - 2 * qk_w - v_w) // 2
    grid = rows // tm
    row_spec = lambda w: pl.BlockSpec((tm, w), lambda i: (i, 0))
    tab_spec = pl.BlockSpec((tm, LANES), lambda i: (i % table_blocks, 0))
    out_shape = (jax.ShapeDtypeStruct((rows, qk_w), BF16), jax.ShapeDtypeStruct((rows, qk_w), F32),
                 jax.ShapeDtypeStruct((rows, qk_w), BF16), jax.ShapeDtypeStruct((rows, v_w), F32),
                 jax.ShapeDtypeStruct((rows, v_w), BF16), jax.ShapeDtypeStruct((rows, c_w), F32))
    return pl.pallas_call(
        functools.partial(_front_kernel, qk_w=qk_w, v_w=v_w, c_w=c_w),
        grid=(grid,),
        in_specs=[row_spec(d), pl.BlockSpec((1, d), lambda i: (0, 0)),
                  pl.BlockSpec(w_bf.shape, lambda i: (0, 0)), tab_spec, tab_spec, tab_spec],
        out_specs=(row_spec(qk_w), row_spec(qk_w), row_spec(qk_w), row_spec(v_w), row_spec(v_w), row_spec(c_w)),
        out_shape=out_shape,
        compiler_params=_cparams(("parallel",)),
        name="front",
    )(x, g, w_bf, *tables)


def _attn_kernel(lam_ref, q_ref, k_ref, v_ref, km_ref, vm_ref, g_ref, o_ref, m_ref, l_ref, acc_ref,
                 *, t, out_scale):
    qi = pl.program_id(2)
    q = q_ref[...]
    lane = lax.broadcasted_iota(I32, q.shape, 1)
    zero = jnp.zeros_like(q)
    qc = (jnp.where(lane < D_HEAD, q, zero), jnp.where(lane >= D_HEAD, q, zero))
    nt = (((1,), (1,)), ((), ()))
    rep = t // LANES

    km, vm = km_ref[...], vm_ref[...]
    for c in range(2):
        s = lax.dot_general(qc[c], km, nt, preferred_element_type=F32)
        m = jnp.max(s, axis=1, keepdims=True)
        p = jnp.exp(s - m)
        m_ref[c] = jnp.broadcast_to(m, (t, LANES))
        l_ref[c] = jnp.broadcast_to(jnp.sum(p, axis=1, keepdims=True), (t, LANES))
        acc_ref[c] = jnp.dot(p.astype(BF16), vm, preferred_element_type=F32)

    def step(j, masked):
        start = pl.multiple_of(j * t, t)
        kblk = k_ref[pl.ds(start, t), :]
        vblk = v_ref[pl.ds(start, t), :]
        for c in range(2):
            s = lax.dot_general(qc[c], kblk, nt, preferred_element_type=F32)
            if masked:
                row = lax.broadcasted_iota(I32, s.shape, 0)
                col = lax.broadcasted_iota(I32, s.shape, 1)
                s = jnp.where(row >= col, s, -jnp.inf)
            m_prev = m_ref[c]
            m_new = jnp.maximum(m_prev, jnp.max(s, axis=1, keepdims=True))
            alpha = jnp.exp(m_prev - m_new)
            p = jnp.exp(s - jnp.concatenate([m_new] * rep, axis=1))
            l_ref[c] = alpha * l_ref[c] + jnp.sum(p, axis=1, keepdims=True)
            acc_ref[c] = alpha * acc_ref[c] + jnp.dot(p.astype(BF16), vblk, preferred_element_type=F32)
            m_ref[c] = m_new

    def body(j, carry):
        step(j, False)
        return carry

    lax.fori_loop(0, qi, body, 0)
    step(qi, True)

    o = acc_ref[0] / l_ref[0] - lam_ref[0, 0] * (acc_ref[1] / l_ref[1])
    ms = jnp.mean(o * o, axis=1, keepdims=True)
    o_ref[...] = ((o * lax.rsqrt(ms + RMS_EPS)) * g_ref[...] * out_scale).astype(BF16)


def _attn_prompt(lam, q, kb, vb, kmb, vmb, subln_g, batch, seq, out_scale):
    t = ATTN_T
    nq = seq // t
    blk = lambda b, h, i: (b * nq + i, h)
    return pl.pallas_call(
        functools.partial(_attn_kernel, t=t, out_scale=out_scale),
        grid=(batch, N_HEADS, nq),
        in_specs=[pl.BlockSpec(memory_space=pltpu.SMEM),
                  pl.BlockSpec((t, LANES), blk),
                  pl.BlockSpec((seq, LANES), lambda b, h, i: (b, h)),
                  pl.BlockSpec((seq, LANES), lambda b, h, i: (b, h)),
                  pl.BlockSpec((N_META, LANES), lambda b, h, i: (0, h)),
                  pl.BlockSpec((N_META, LANES), lambda b, h, i: (0, h)),
                  pl.BlockSpec((1, LANES), lambda b, h, i: (0, 0))],
        out_specs=pl.BlockSpec((t, LANES), blk),
        out_shape=jax.ShapeDtypeStruct(q.shape, BF16),
        scratch_shapes=[pltpu.VMEM((2, t, LANES), F32), pltpu.VMEM((2, t, LANES), F32),
                        pltpu.VMEM((2, t, LANES), F32)],
        compiler_params=_cparams(("parallel", "parallel", "arbitrary")),
        name="attn_prompt",
    )(lam, q, kb, vb, kmb, vmb, subln_g)


def _decode_kernel(pt_ref, lam_ref, qbd_ref, kn_ref, vn_ref, g_ref, k_hbm, v_hbm,
                   o_ref, m_ref, l_ref, acc_ref, kbuf, vbuf, sem, *, out_scale):
    b, j = pl.program_id(0), pl.program_id(1)
    n_b, n_j = pl.num_programs(0), pl.num_programs(1)
    n_pages = kbuf.shape[1]
    n_hc = qbd_ref.shape[0]
    width = qbd_ref.shape[1]
    rep = width // LANES
    step = b * n_j + j
    slot = step % 2

    def issue(bb, jj, s):
        for i in range(n_pages):
            page = pt_ref[bb, jj * n_pages + i]
            pltpu.make_async_copy(k_hbm.at[page], kbuf.at[s, i], sem.at[0, s]).start()
            pltpu.make_async_copy(v_hbm.at[page], vbuf.at[s, i], sem.at[1, s]).start()

    @pl.when(step == 0)
    def _():
        issue(0, 0, 0)

    @pl.when(step + 1 < n_b * n_j)
    def _():
        wrap = j + 1 == n_j
        issue(jnp.where(wrap, b + 1, b), jnp.where(wrap, 0, j + 1), 1 - slot)

    pltpu.make_async_copy(k_hbm.at[pl.ds(0, n_pages)], kbuf.at[slot], sem.at[0, slot]).wait()
    pltpu.make_async_copy(v_hbm.at[pl.ds(0, n_pages)], vbuf.at[slot], sem.at[1, slot]).wait()

    qf = qbd_ref[...]
    qb = qf.astype(BF16)

    @pl.when(j == 0)
    def _():
        m_ref[...] = jnp.full(m_ref.shape, -jnp.inf, F32)
        l_ref[...] = jnp.zeros(l_ref.shape, F32)
        acc_ref[...] = jnp.zeros(acc_ref.shape, F32)

    ss = [jnp.dot(qb, kbuf[slot, i].astype(BF16), preferred_element_type=F32)
          for i in range(n_pages)]
    m_cur = ss[0].max(axis=1, keepdims=True)
    for s in ss[1:]:
        m_cur = jnp.maximum(m_cur, s.max(axis=1, keepdims=True))
    m_prev = m_ref[...]
    m_new = jnp.maximum(m_prev, m_cur)
    alpha = jnp.exp(m_prev - m_new)
    lsum = jnp.zeros((n_hc, 1), F32)
    pv = jnp.zeros((n_hc, width), F32)
    for i in range(n_pages):
        p = jnp.exp(ss[i] - m_new)
        lsum = lsum + jnp.sum(p, axis=1, keepdims=True)
        pb = p.astype(BF16)
        pv = pv + jnp.concatenate(
            [jnp.dot(pb, vbuf[slot, i, pl.ds(h, PAGE_SIZE, stride=N_HEADS), :].astype(BF16),
                     preferred_element_type=F32) for h in range(N_HEADS)], axis=1)
    l_new = alpha * l_ref[...] + lsum
    acc_new = jnp.concatenate([alpha] * rep, axis=1) * acc_ref[...] + pv
    m_ref[...] = m_new
    l_ref[...] = l_new
    acc_ref[...] = acc_new

    @pl.when(j == pl.num_programs(1) - 1)
    def _():
        s_new = jnp.sum(qf * kn_ref[...], axis=1, keepdims=True)
        m_f = jnp.maximum(m_new, s_new)
        a2 = jnp.exp(m_new - m_f)
        p_new = jnp.exp(s_new - m_f)
        l_f = a2 * l_new + p_new
        acc_f = jnp.concatenate([a2] * rep, axis=1) * acc_new + jnp.concatenate([p_new] * rep, axis=1) * vn_ref[...]
        o = acc_f / jnp.concatenate([l_f] * rep, axis=1)
        row = lax.broadcasted_iota(I32, o.shape, 0)
        ln = lax.broadcasted_iota(I32, o.shape, 1)
        own = (ln // D_VHEAD) == (row // 2)
        coef = jnp.where(row % 2 == 0, 1.0, -lam_ref[0, 0])
        od = jnp.sum(jnp.where(own, o * coef, 0.0), axis=0, keepdims=True)
        parts = []
        for h in range(rep):
            oh = od[:, h * D_VHEAD:(h + 1) * D_VHEAD]
            ms = jnp.mean(oh * oh, axis=1, keepdims=True)
            parts.append(oh * lax.rsqrt(ms + RMS_EPS))
        o_ref[...] = jnp.concatenate(parts, axis=1) * g_ref[...] * out_scale


def _attn_decode(page_table, lam, qbd, k_new, v_new, g_row, cache_k, cache_v, out_scale):
    bd, n_hc, width = qbd.shape
    n_steps = page_table.shape[1] // DEC_PAGES
    row_spec = pl.BlockSpec((None, 1, width), lambda b, j, pt: (b, 0, 0))
    grid_spec = pltpu.PrefetchScalarGridSpec(
        num_scalar_prefetch=1,
        grid=(bd, n_steps),
        in_specs=[pl.BlockSpec(memory_space=pltpu.SMEM),
                  pl.BlockSpec((None, n_hc, width), lambda b, j, pt: (b, 0, 0)),
                  row_spec, row_spec,
                  pl.BlockSpec((1, width), lambda b, j, pt: (0, 0)),
                  pl.BlockSpec(memory_space=pl.ANY), pl.BlockSpec(memory_space=pl.ANY)],
        out_specs=row_spec,
        scratch_shapes=[pltpu.VMEM((n_hc, LANES), F32), pltpu.VMEM((n_hc, LANES), F32),
                        pltpu.VMEM((n_hc, width), F32),
                        pltpu.VMEM((2, DEC_PAGES) + cache_k.shape[1:], F32),
                        pltpu.VMEM((2, DEC_PAGES) + cache_v.shape[1:], F32),
                        pltpu.SemaphoreType.DMA((2, 2))],
    )
    return pl.pallas_call(
        functools.partial(_decode_kernel, out_scale=out_scale),
        grid_spec=grid_spec,
        out_shape=jax.ShapeDtypeStruct((bd, 1, width), F32),
        compiler_params=_cparams(("arbitrary", "arbitrary")),
        name="attn_decode",
    )(page_table, lam, qbd, k_new, v_new, g_row, cache_k, cache_v)


def _ln_swish(y, g, b):
    mu = jnp.mean(y, axis=-1, keepdims=True)
    var = jnp.mean(jnp.square(y - mu), axis=-1, keepdims=True)
    z = (y - mu) * lax.rsqrt(var + LN_EPS) * g + b
    return z * jax.nn.sigmoid(z)


def _conv_prompt_kernel(cur_ref, prev_ref, hist_ref, w_ref, b_ref, g_ref, beta_ref, o_ref, buf_ref):
    i = pl.program_id(1)
    t = cur_ref.shape[0]
    buf_ref[0:CONV_HIST, :] = jnp.where(i == 0, hist_ref[...], prev_ref[...])
    buf_ref[CONV_HIST:CONV_HIST + t, :] = cur_ref[...]
    off = CONV_HIST - (CONV_WIDTH - 1)

    def chunk(r, carry):
        r0 = pl.multiple_of(r * CONV_CHUNK, CONV_CHUNK)
        win = buf_ref[pl.ds(r0, CONV_CHUNK + CONV_HIST), :]
        acc = jnp.zeros((CONV_CHUNK, cur_ref.shape[1]), F32)
        for s in range(8):
            taps = [w for w in range(CONV_WIDTH) if (off + w) % 8 == s]
            ws = win if s == 0 else win[s:s + CONV_CHUNK + CONV_HIST - 8, :]
            for w in taps:
                base = off + w - s
                acc = acc + ws[base:base + CONV_CHUNK, :] * w_ref[w:w + 1, :]
        y = acc + b_ref[...]
        o_ref[pl.ds(r0, CONV_CHUNK), :] = _ln_swish(y, g_ref[...], beta_ref[...]).astype(o_ref.dtype)
        return carry

    lax.fori_loop(0, t // CONV_CHUNK, chunk, 0)


def _conv_prompt(glu, hist, w, b, g, beta):
    batch, seq, c = glu.shape
    t = CONV_T
    per = t // CONV_HIST
    vec = pl.BlockSpec((1, c), lambda bb, i: (0, 0))
    return pl.pallas_call(
        _conv_prompt_kernel,
        grid=(batch, seq // t),
        in_specs=[pl.BlockSpec((None, t, c), lambda bb, i: (bb, i, 0)),
                  pl.BlockSpec((None, CONV_HIST, c), lambda bb, i: (bb, jnp.maximum(i * per - 1, 0), 0)),
                  pl.BlockSpec((CONV_HIST, c), lambda bb, i: (0, 0)),
                  pl.BlockSpec(w.shape, lambda bb, i: (0, 0)), vec, vec, vec],
        out_specs=pl.BlockSpec((None, t, c), lambda bb, i: (bb, i, 0)),
        out_shape=jax.ShapeDtypeStruct(glu.shape, BF16),
        scratch_shapes=[pltpu.VMEM((CONV_HIST + t, c), F32)],
        compiler_params=_cparams(("parallel", "parallel")),
        name="conv_prompt",
    )(glu, glu, hist, w, b, g, beta)


def _conv_sample_kernel(st_ref, glu_ref, w_ref, b_ref, g_ref, beta_ref, o_ref):
    n_hist = st_ref.shape[0]
    y = glu_ref[...] * w_ref[n_hist:n_hist + 1, :] + b_ref[...]
    for w in range(n_hist):
        y = y + st_ref[w] * w_ref[w:w + 1, :]
    o_ref[...] = _ln_swish(y, g_ref[...], beta_ref[...]).astype(o_ref.dtype)


def _conv_sample(state, glu, w, b, g, beta):
    n_hist, bd, c = state.shape
    tb = 16
    vec = pl.BlockSpec((1, c), lambda i: (0, 0))
    return pl.pallas_call(
        _conv_sample_kernel,
        grid=(bd // tb,),
        in_specs=[pl.BlockSpec((n_hist, tb, c), lambda i: (0, i, 0)),
                  pl.BlockSpec((tb, c), lambda i: (i, 0)),
                  pl.BlockSpec(w.shape, lambda i: (0, 0)), vec, vec, vec],
        out_specs=pl.BlockSpec((tb, c), lambda i: (i, 0)),
        out_shape=jax.ShapeDtypeStruct((bd, c), BF16),
        compiler_params=_cparams(("parallel",)),
        name="conv_sample",
    )(state, glu, w, b, g, beta)


def _post_rows(n, oa, yc, x, wo_ref, g2_ref, wrh_ref, wrl_ref, br_ref, tri_ref,
               x1_ref, ht_ref, route_ref, run_ref):
    half = oa.shape[1]
    x1 = (x + jnp.dot(oa, wo_ref[0:half, :], preferred_element_type=F32)
          + jnp.dot(yc, wo_ref[half:2 * half, :], preferred_element_type=F32))
    x1_ref[0:n, :] = x1
    ms = jnp.mean(x1 * x1, axis=-1, keepdims=True)
    h = (x1 * lax.rsqrt(ms + RMS_EPS)) * g2_ref[...]
    for s in range(ROW_TILE):
        ht_ref[pl.ds(s, n, stride=ROW_TILE), :] = h[:, s * LANES:(s + 1) * LANES]

    h_hi = h.astype(BF16)
    h_lo = (h - h_hi.astype(F32)).astype(BF16)
    logits = (jnp.dot(h_hi, wrh_ref[...], preferred_element_type=F32)
              + jnp.dot(h_lo, wrh_ref[...], preferred_element_type=F32)
              + jnp.dot(h_hi, wrl_ref[...], preferred_element_type=F32)) + br_ref[...]
    lane = lax.broadcasted_iota(I32, logits.shape, 1)
    work = jnp.where(lane < N_EXPERTS, logits, -jnp.inf)
    vals, idxs = [], []
    onehot = jnp.zeros(logits.shape, F32)
    for _ in range(TOP_K):
        m = jnp.max(work, axis=1, keepdims=True)
        idx = jnp.min(jnp.where(work == m, lane, LANES), axis=1, keepdims=True)
        sel = lane == idx
        vals.append(m)
        idxs.append(idx)
        onehot = jnp.where(sel, 1.0, onehot)
        work = jnp.where(sel, -jnp.inf, work)
    exps = [jnp.exp(v - vals[0]) for v in vals]
    denom = exps[0] + exps[1] + exps[2] + exps[3]
    gates = [e / denom for e in exps]

    before = jnp.dot(tri_ref[0:n, 0:n], onehot.astype(BF16), preferred_element_type=F32) + run_ref[...]
    route = jnp.zeros(logits.shape, F32)
    for k in range(TOP_K):
        rank = jnp.sum(jnp.where(lane == idxs[k], before, 0.0), axis=1, keepdims=True)
        route = jnp.where(lane == k, idxs[k].astype(F32), route)
        route = jnp.where(lane == TOP_K + k, rank, route)
        route = jnp.where(lane == 2 * TOP_K + k, gates[k], route)
    route_ref[0:n, :] = route
    run_ref[...] = run_ref[...] + jnp.sum(onehot, axis=0, keepdims=True)


def _post_kernel(oap_ref, ycp_ref, xp_ref, oas_ref, ycs_ref, xs_ref,
                 wo_ref, g2_ref, wrh_ref, wrl_ref, br_ref, tri_ref,
                 x1_ref, ht_ref, route_ref, cnt_ref, run_ref):
    i = pl.program_id(0)
    last = pl.num_programs(0) - 1
    tm = xp_ref.shape[0]
    ns = xs_ref.shape[0]
    shared = (wo_ref, g2_ref, wrh_ref, wrl_ref, br_ref, tri_ref, x1_ref, ht_ref, route_ref, run_ref)

    @pl.when(i == 0)
    def _():
        run_ref[...] = jnp.zeros(run_ref.shape, F32)

    @pl.when(i < last)
    def _():
        _post_rows(tm, oap_ref[...], ycp_ref[...], xp_ref[...], *shared)

    @pl.when(i == last)
    def _():
        _post_rows(ns, oas_ref[...], ycs_ref[...], xs_ref[...], *shared)
        x1_ref[ns:tm, :] = jnp.zeros((tm - ns, x1_ref.shape[1]), F32)
        route_ref[ns:tm, :] = jnp.zeros((tm - ns, LANES), F32)
        ht_ref[ns * ROW_TILE:tm * ROW_TILE, :] = jnp.zeros(((tm - ns) * ROW_TILE, LANES), F32)
        cnt_ref[...] = run_ref[...]


def _post(oa_p, yc_p, x_p, oa_s, yc_s, x_s, wo_bf, g2, wr_hi, wr_lo, br):
    rows, d = x_p.shape
    ns = x_s.shape[0]
    tm = POST_TM
    n_steps = rows // tm + 1
    n_pad = n_steps * tm
    half = oa_p.shape[1]
    tri = (np.arange(tm)[:, None] > np.arange(tm)[None, :]).astype(np.float32)
    tri = jnp.asarray(tri, BF16)
    last_p = rows // tm - 1
    p_spec = lambda w: pl.BlockSpec((tm, w), lambda i: (jnp.minimum(i, last_p), 0))
    full = lambda a: pl.BlockSpec(a.shape, lambda i: (0, 0))
    out_spec = lambda r, w: pl.BlockSpec((r, w), lambda i: (i, 0))
    return pl.pallas_call(
        _post_kernel,
        grid=(n_steps,),
        in_specs=[p_spec(half), p_spec(half), p_spec(d), full(oa_s), full(yc_s), full(x_s),
                  full(wo_bf), full(g2), full(wr_hi), full(wr_lo), full(br), full(tri)],
        out_specs=(out_spec(tm, d), out_spec(tm * ROW_TILE, LANES), out_spec(tm, LANES),
                   pl.BlockSpec((1, LANES), lambda i: (0, 0))),
        out_shape=(jax.ShapeDtypeStruct((n_pad, d), F32), jax.ShapeDtypeStruct((n_pad * ROW_TILE, LANES), F32),
                   jax.ShapeDtypeStruct((n_pad, LANES), F32), jax.ShapeDtypeStruct((1, LANES), F32)),
        scratch_shapes=[pltpu.VMEM((1, LANES), F32)],
        compiler_params=_cparams(("arbitrary",)),
        name="post",
    )(oa_p, yc_p, x_p, oa_s, yc_s, x_s, wo_bf, g2, wr_hi, wr_lo, br, tri)


def _moe_kernel(be_ref, nv_ref, tok_cur_ref, tok_nxt_ref, h_hbm,
                wg_ref, wu_ref, wd_ref, bg_ref, bu_ref, bd_ref, ys_ref,
                xbuf, sem, wgb, wub, wdb):
    i = pl.program_id(0)
    nv = nv_ref[0]
    tb = xbuf.shape[1] // ROW_TILE
    slot = i % 2

    def issue(row_ref, s):
        for r in range(tb):
            src = h_hbm.at[pl.ds(pl.multiple_of(row_ref[r], ROW_TILE), ROW_TILE)]
            pltpu.make_async_copy(src, xbuf.at[s, pl.ds(r * ROW_TILE, ROW_TILE)],
                                  sem.at[s]).start(priority=r % N_DMA_QUEUES)

    def wait(s):
        pltpu.make_async_copy(h_hbm.at[pl.ds(0, tb * ROW_TILE)], xbuf.at[s], sem.at[s]).wait()

    @pl.when(i == 0)
    def _():
        issue(tok_cur_ref, 0)

    @pl.when(i < nv)
    def _():
        changed = jnp.logical_or(i == 0, be_ref[i] != be_ref[jnp.maximum(i - 1, 0)])

        @pl.when(changed)
        def _():
            wgb[...] = wg_ref[...].astype(BF16)
            wub[...] = wu_ref[...].astype(BF16)
            wdb[...] = wd_ref[...].astype(BF16)

        wait(slot)
        x = jnp.concatenate([xbuf[slot, pl.ds(s, tb, stride=ROW_TILE), :] for s in range(ROW_TILE)],
                            axis=1).astype(BF16)
        issue(tok_nxt_ref, 1 - slot)

        g = jnp.minimum(jnp.dot(x, wgb[...], preferred_element_type=F32) + bg_ref[...], SWIGLU_LIMIT)
        u = jnp.clip(jnp.dot(x, wub[...], preferred_element_type=F32) + bu_ref[...], -SWIGLU_LIMIT, SWIGLU_LIMIT)
        a = (g * jax.nn.sigmoid(SWIGLU_ALPHA * g) * (u + 1.0)).astype(BF16)
        y = jnp.dot(a, wdb[...], preferred_element_type=F32) + bd_ref[...]
        for s in range(ROW_TILE):
            ys_ref[pl.ds(s, tb, stride=ROW_TILE), :] = y[:, s * LANES:(s + 1) * LANES]

    @pl.when(i == nv)
    def _():
        wait(slot)

    @pl.when(i >= nv)
    def _():
        ys_ref[...] = jnp.zeros(ys_ref.shape, F32)


def _moe(block_e, n_valid, slot_row, h_tiles, wg, wu, wd, bg, bu, bd):
    p = slot_row.shape[0]
    tb = MOE_TB
    nb = p // tb
    n_e, d, de = wg.shape
    assert d == ROW_TILE * LANES
    w_spec = lambda r, c: pl.BlockSpec((None, r, c), lambda i, be, nv: (be[i], 0, 0))
    b_spec = lambda c: pl.BlockSpec((None, 1, c), lambda i, be, nv: (be[i], 0, 0))
    last = lambda i, nv: jnp.minimum(i, nv[0] - 1)
    grid_spec = pltpu.PrefetchScalarGridSpec(
        num_scalar_prefetch=2,
        grid=(nb,),
        in_specs=[pl.BlockSpec((tb,), lambda i, be, nv: (last(i, nv),), memory_space=pltpu.SMEM),
                  pl.BlockSpec((tb,), lambda i, be, nv: (last(i + 1, nv),), memory_space=pltpu.SMEM),
                  pl.BlockSpec(memory_space=pl.ANY),
                  w_spec(d, de), w_spec(d, de), w_spec(de, d), b_spec(de), b_spec(de), b_spec(d)],
        out_specs=pl.BlockSpec((tb * ROW_TILE, LANES), lambda i, be, nv: (i, 0)),
        scratch_shapes=[pltpu.VMEM((2, tb * ROW_TILE, LANES), F32), pltpu.SemaphoreType.DMA((2,)),
                        pltpu.VMEM((d, de), BF16), pltpu.VMEM((d, de), BF16), pltpu.VMEM((de, d), BF16)],
    )
    return pl.pallas_call(
        _moe_kernel,
        grid_spec=grid_spec,
        out_shape=jax.ShapeDtypeStruct((p * ROW_TILE, LANES), F32),
        compiler_params=_cparams(("arbitrary",)),
        name="moe",
    )(block_e, n_valid, slot_row, slot_row, h_tiles, wg, wu, wd,
      bg.reshape(n_e, 1, de), bu.reshape(n_e, 1, de), bd.reshape(n_e, 1, d))


def _combine_kernel(pos_cur_ref, pos_nxt_ref, ys_hbm, x1_ref, route_ref, fg_ref, yp_ref, ys_ref, buf, sem):
    i = pl.program_id(0)
    n = pl.num_programs(0)
    t = x1_ref.shape[0]
    slot = i % 2

    def issue(pos_ref, s):
        for r in range(t):
            for k in range(TOP_K):
                src = ys_hbm.at[pl.ds(pl.multiple_of(pos_ref[r * TOP_K + k], ROW_TILE), ROW_TILE)]
                pltpu.make_async_copy(src, buf.at[s, k, pl.ds(r * ROW_TILE, ROW_TILE)],
                                      sem.at[s]).start(priority=k % N_DMA_QUEUES)

    def wait(s):
        for k in range(TOP_K):
            pltpu.make_async_copy(ys_hbm.at[pl.ds(0, t * ROW_TILE)], buf.at[s, k], sem.at[s]).wait()

    @pl.when(i == 0)
    def _():
        issue(pos_cur_ref, 0)

    issue(pos_nxt_ref, 1 - slot)
    wait(slot)
    route = route_ref[...]
    gates = [route[:, 2 * TOP_K + k:2 * TOP_K + k + 1] for k in range(TOP_K)]
    parts = []
    for s in range(ROW_TILE):
        acc = x1_ref[:, s * LANES:(s + 1) * LANES]
        for k in range(TOP_K):
            acc = acc + gates[k] * buf[slot, k, pl.ds(s, t, stride=ROW_TILE), :]
        parts.append(acc)
    y = jnp.concatenate(parts, axis=1)
    ms = jnp.mean(y * y, axis=-1, keepdims=True)
    out = (y * lax.rsqrt(ms + RMS_EPS)) * fg_ref[...]

    @pl.when(i < n - 1)
    def _():
        yp_ref[...] = out

    @pl.when(i == n - 1)
    def _():
        ys_ref[...] = out
        wait(1 - slot)


def _combine(pos_row, ys_tiles, x1, route, fg, n_prompt, n_sample):
    d = x1.shape[1]
    t = COMB_T
    assert n_sample == t and n_prompt % t == 0
    n = n_prompt // t + 1
    return pl.pallas_call(
        _combine_kernel,
        grid=(n,),
        in_specs=[pl.BlockSpec((t * TOP_K,), lambda i: (i,), memory_space=pltpu.SMEM),
                  pl.BlockSpec((t * TOP_K,), lambda i: (jnp.minimum(i + 1, n - 1),), memory_space=pltpu.SMEM),
                  pl.BlockSpec(memory_space=pl.ANY),
                  pl.BlockSpec((t, d), lambda i: (i, 0)),
                  pl.BlockSpec((t, LANES), lambda i: (i, 0)),
                  pl.BlockSpec((1, d), lambda i: (0, 0))],
        out_specs=(pl.BlockSpec((t, d), lambda i: (jnp.minimum(i, n - 2), 0)),
                   pl.BlockSpec((t, d), lambda i: (0, 0))),
        out_shape=(jax.ShapeDtypeStruct((n_prompt, d), F32), jax.ShapeDtypeStruct((n_sample, d), F32)),
        scratch_shapes=[pltpu.VMEM((2, TOP_K, t * ROW_TILE, LANES), F32), pltpu.SemaphoreType.DMA((2,))],
        compiler_params=_cparams(("arbitrary",)),
        name="combine",
    )(pos_row, pos_row, ys_tiles, x1, route, fg)


def _invert_kernel(pos_ref, init_ref, out_ref):
    pltpu.sync_copy(init_ref, out_ref)
    n = pos_ref.shape[0]

    def body(j, carry):
        base = j * INVERT_UNROLL
        row0 = base * (ROW_TILE // TOP_K)
        for u in range(INVERT_UNROLL):
            out_ref[pos_ref[base + u]] = row0 + (u // TOP_K) * ROW_TILE
        return carry

    lax.fori_loop(0, n // INVERT_UNROLL, body, 0)


def _invert(pos_flat, fill_row, n_slots):
    init = jnp.full((n_slots,), fill_row, I32)
    smem = pl.BlockSpec(memory_space=pltpu.SMEM)
    return pl.pallas_call(
        _invert_kernel,
        in_specs=[smem, smem],
        out_specs=smem,
        out_shape=jax.ShapeDtypeStruct((n_slots,), I32),
        name="invert",
    )(pos_flat, init)


def _routing_tables(route, counts, n_tok):
    tb = MOE_TB
    a = n_tok * TOP_K
    idx = route[:, 0:TOP_K].astype(I32)
    rank = route[:, TOP_K:2 * TOP_K].astype(I32)
    cnt = counts[0, :N_EXPERTS].astype(I32)
    padded = ((cnt + tb - 1) // tb) * tb
    ends = jnp.cumsum(padded)
    pad_start = ends - padded
    onehot = idx[..., None] == jnp.arange(N_EXPERTS, dtype=I32)
    pos = jnp.sum(jnp.where(onehot, pad_start, 0), axis=-1) + rank
    nb = a // tb + N_EXPERTS
    pos_flat = pos.reshape(a)
    slot_row = _invert(pos_flat, n_tok * ROW_TILE, nb * tb)
    block_start = jnp.arange(nb, dtype=I32) * tb
    n_valid = (ends[-1] // tb).astype(I32)
    be = jnp.sum(block_start[:, None] >= ends[None, :], axis=1).astype(I32)
    be_last = jnp.sum(jnp.where(jnp.arange(nb) == n_valid - 1, be, 0))
    block_e = jnp.where(jnp.arange(nb) < n_valid, jnp.minimum(be, N_EXPERTS - 1), be_last).astype(I32)
    return pos_flat * ROW_TILE, slot_row, block_e, n_valid.reshape(1)


def kernel(x_prompt, x_sample, cache_k, cache_v, page_table, state_conv, meta_tokens, norm1_g, w_in,
           lambda_q1, lambda_k1, lambda_q2, lambda_k2, subln_g, conv_w, conv_b, conv_ln_g, conv_ln_b,
           w_out, norm2_g, w_router, b_router, w_gate, b_gate, w_up, b_up, w_down, b_down, final_norm_g):
    depth = norm1_g.shape[0]
    assert depth == 1, "single trunk layer"
    layer = 0
    batch, seq, d = x_prompt.shape
    bd, dec_seq, _ = x_sample.shape
    assert dec_seq == 1
    past_len = page_table.shape[1] * PAGE_SIZE
    qk_w = N_HEADS * 2 * D_HEAD
    v_w = N_HEADS * D_VHEAD
    c_conv = conv_w.shape[2]
    out_scale = 1.0 - _lambda_init(layer)

    lam = (jnp.exp(jnp.sum(lambda_q1[layer] * lambda_k1[layer]))
           - jnp.exp(jnp.sum(lambda_q2[layer] * lambda_k2[layer])) + _lambda_init(layer)).reshape(1, 1).astype(F32)
    w_in_bf = w_in[layer].astype(BF16)
    g1 = norm1_g[layer].reshape(1, d)

    tab_p = _rope_tables(np.arange(N_META, N_META + seq))
    q_p, k_p, kb_p, v_p, vb_p, glu_p = _front(x_prompt.reshape(batch * seq, d), g1, w_in_bf, tab_p,
                                              FRONT_TM, seq // FRONT_TM)
    x_sm = jnp.concatenate([x_sample.reshape(bd, d), meta_tokens.astype(F32)], axis=0)
    tab_sm = _rope_tables(np.concatenate([np.full(bd, past_len), np.arange(N_META)]))
    q_sm, k_sm, kb_sm, v_sm, vb_sm, glu_sm = _front(x_sm, g1, w_in_bf, tab_sm, bd + N_META, 1)

    sub_g = subln_g[layer].reshape(1, D_VHEAD)
    oa_p = _attn_prompt(lam, q_p, kb_p, vb_p, kb_sm[bd:], vb_sm[bd:], sub_g, batch, seq, out_scale)

    n_hc = 2 * N_HEADS
    hc_mask = (jnp.arange(qk_w)[None, :] // D_HEAD) == jnp.arange(n_hc)[:, None]
    qbd = jnp.where(hc_mask[None], q_sm[:bd].astype(F32)[:, None, :], 0.0)
    n_phys = cache_k.shape[1]
    kt = jnp.transpose(cache_k[layer], (0, 2, 3, 4, 1)).reshape(n_phys, qk_w, PAGE_SIZE)
    vr = cache_v[layer].reshape(n_phys, PAGE_SIZE * N_HEADS, D_VHEAD)
    oa_s = _attn_decode(page_table, lam, qbd, k_sm[:bd].reshape(bd, 1, qk_w), v_sm[:bd].reshape(bd, 1, v_w),
                        jnp.tile(sub_g, (1, N_HEADS)), kt, vr, out_scale)

    cw = jnp.concatenate([conv_w[layer], jnp.zeros((CONV_HIST - CONV_WIDTH, c_conv), F32)], axis=0)
    cb, cg, cbeta = conv_b[layer].reshape(1, -1), conv_ln_g[layer].reshape(1, -1), conv_ln_b[layer].reshape(1, -1)
    hist = jnp.concatenate([jnp.zeros((CONV_HIST - N_META, c_conv), F32), glu_sm[bd:]], axis=0)
    yc_p = _conv_prompt(glu_p.reshape(batch, seq, c_conv), hist, cw, cb, cg, cbeta)
    yc_s = _conv_sample(jnp.transpose(state_conv[layer], (1, 0, 2)), glu_sm[:bd], cw, cb, cg, cbeta)

    n_prompt = batch * seq
    n_tok = n_prompt + bd
    wr = jnp.zeros((d, LANES), F32).at[:, :N_EXPERTS].set(w_router[layer])
    wr_hi = wr.astype(BF16)
    wr_lo = (wr - wr_hi.astype(F32)).astype(BF16)
    br = jnp.zeros((1, LANES), F32).at[0, :N_EXPERTS].set(b_router[layer])
    x1, h_tiles, route, counts = _post(oa_p, yc_p.reshape(n_prompt, c_conv), x_prompt.reshape(n_prompt, d),
                                       oa_s.reshape(bd, v_w).astype(BF16), yc_s, x_sample.reshape(bd, d),
                                       w_out[layer].astype(BF16), norm2_g[layer].reshape(1, d), wr_hi, wr_lo, br)

    pos_row, slot_row, block_e, n_valid = _routing_tables(route[:n_tok], counts, n_tok)
    ys_tiles = _moe(block_e, n_valid, slot_row, h_tiles, w_gate[layer], w_up[layer], w_down[layer],
                    b_gate[layer], b_up[layer], b_down[layer])
    y_p, y_s = _combine(pos_row, ys_tiles, x1, route, final_norm_g.reshape(1, d), n_prompt, bd)

    y_prompt = y_p.reshape(batch, seq, d)
    y_sample = y_s.reshape(bd, 1, d)
    k_meta = jnp.broadcast_to(k_sm[bd:][None], (batch, N_META, qk_w))
    v_meta = jnp.broadcast_to(v_sm[bd:][None], (batch, N_META, v_w))
    k_prompt = jnp.concatenate([k_meta, k_p.reshape(batch, seq, qk_w)], axis=1)
    v_prompt = jnp.concatenate([v_meta, v_p.reshape(batch, seq, v_w)], axis=1)
    k_prompt = k_prompt.reshape(1, batch, seq + N_META, N_HEADS, 2, D_HEAD)
    v_prompt = v_prompt.reshape(1, batch, seq + N_META, N_HEADS, D_VHEAD)
    n_keep = CONV_WIDTH - 1
    conv_prompt = glu_p.reshape(batch, seq, c_conv)[:, seq - n_keep:][None]
    k_sample = k_sm[:bd].reshape(1, bd, 1, N_HEADS, 2, D_HEAD)
    v_sample = v_sm[:bd].reshape(1, bd, 1, N_HEADS, D_VHEAD)
    conv_sample = jnp.concatenate([state_conv[layer][:, 1:], glu_sm[:bd][:, None, :]], axis=1)[None]
    return (y_prompt, y_sample, k_prompt, v_prompt, conv_prompt, k_sample, v_sample, conv_sample)
```

```python
import functools
import math

import numpy as np
import jax
import jax.numpy as jnp
from jax import lax
from jax.experimental import pallas as pl
from jax.experimental.pallas import tpu as pltpu

F32 = jnp.float32
BF16 = jnp.bfloat16
I32 = jnp.int32

N_META = 16
N_HEADS = 4
D_HEAD = 64
D_VHEAD = 128
ROT_DIM = 16
ROPE_THETA = 500000.0
CONV_WIDTH = 31
N_EXPERTS = 32
TOP_K = 4
SWIGLU_LIMIT = 7.0
SWIGLU_ALPHA = 1.702
RMS_EPS = 1e-6
LN_EPS = 1e-5
PAGE_SIZE = 128

LANES = 128
ROW_TILE = 8
INVERT_UNROLL = 32
N_DMA_QUEUES = 2
VMEM_LIMIT = 56 * 1024 * 1024

FRONT_TM = 512
ATTN_T = 1024
DEC_PAGES = 32
CONV_T = 512
CONV_CHUNK = 64
CONV_HIST = 32
POST_TM = 512
MOE_TB = 256
COMB_T = 128


def _lambda_init(layer):
    return 0.8 - 0.6 * math.exp(-0.3 * layer)


def _cparams(sem):
    return pltpu.CompilerParams(dimension_semantics=sem, vmem_limit_bytes=VMEM_LIMIT)


def _rope_tables(pos):
    pos = np.asarray(pos, np.float64)
    inv = ROPE_THETA ** (-np.arange(0, ROT_DIM, 2, dtype=np.float64) / ROT_DIM)
    ang = pos[:, None] * inv[None, :]
    d = np.arange(LANES) % D_HEAD
    j = d % (ROT_DIM // 2)
    cos_l, sin_l = np.cos(ang)[:, j], np.sin(ang)[:, j]
    c = np.where(d < ROT_DIM, cos_l, 1.0)
    s1 = np.where(d < ROT_DIM // 2, -sin_l, 0.0)
    s2 = np.where((d >= ROT_DIM // 2) & (d < ROT_DIM), sin_l, 0.0)
    return (jnp.asarray(c, F32), jnp.asarray(s1, F32), jnp.asarray(s2, F32))


def _front_kernel(x_ref, g_ref, w_ref, c_ref, s1_ref, s2_ref,
                  q_ref, k_ref, kb_ref, v_ref, vb_ref, glu_ref, *, qk_w, v_w, c_w):
    x = x_ref[...]
    ms = jnp.mean(x * x, axis=-1, keepdims=True)
    h = ((x * lax.rsqrt(ms + RMS_EPS)) * g_ref[...]).astype(BF16)
    c, s1, s2 = c_ref[...], s1_ref[...], s2_ref[...]

    def proj(lo, width):
        return jnp.dot(h, w_ref[:, lo:lo + width], preferred_element_type=F32)

    def rope(z):
        outs = []
        for g in range(z.shape[1] // LANES):
            zg = z[:, g * LANES:(g + 1) * LANES]
            outs.append(zg * c + pltpu.roll(zg, LANES - ROT_DIM // 2, 1) * s1
                        + pltpu.roll(zg, ROT_DIM // 2, 1) * s2)
        return jnp.concatenate(outs, axis=1)

    q = rope(proj(0, qk_w))
    q_ref[...] = (q * (D_HEAD ** -0.5)).astype(BF16)
    k = rope(proj(qk_w, qk_w))
    k_ref[...] = k
    kb_ref[...] = k.astype(BF16)
    v = proj(2 * qk_w, v_w)
    v_ref[...] = v
    vb_ref[...] = v.astype(BF16)
    a = proj(2 * qk_w + v_w, c_w)
    gate = proj(2 * qk_w + v_w + c_w, c_w)
    glu_ref[...] = a * jax.nn.sigmoid(gate)


def _front(x, g, w_bf, tables, tm, table_blocks):
    rows, d = x.shape
    qk_w = N_HEADS * 2 * D_HEAD
    v_w = N_HEADS * D_VHEAD
    c_w = (w_bf.shape[1] - 2 * qk_w - v_w) // 2
    grid = rows // tm
    row_spec = lambda w: pl.BlockSpec((tm, w), lambda i: (i, 0))
    tab_spec = pl.BlockSpec((tm, LANES), lambda i: (i % table_blocks, 0))
    out_shape = (jax.ShapeDtypeStruct((rows, qk_w), BF16), jax.ShapeDtypeStruct((rows, qk_w), F32),
                 jax.ShapeDtypeStruct((rows, qk_w), BF16), jax.ShapeDtypeStruct((rows, v_w), F32),
                 jax.ShapeDtypeStruct((rows, v_w), BF16), jax.ShapeDtypeStruct((rows, c_w), F32))
    return pl.pallas_call(
        functools.partial(_front_kernel, qk_w=qk_w, v_w=v_w, c_w=c_w),
        grid=(grid,),
        in_specs=[row_spec(d), pl.BlockSpec((1, d), lambda i: (0, 0)),
                  pl.BlockSpec(w_bf.shape, lambda i: (0, 0)), tab_spec, tab_spec, tab_spec],
        out_specs=(row_spec(qk_w), row_spec(qk_w), row_spec(qk_w), row_spec(v_w), row_spec(v_w), row_spec(c_w)),
        out_shape=out_shape,
        compiler_params=_cparams(("parallel",)),
        name="front",
    )(x, g, w_bf, *tables)


def _attn_kernel(lam_ref, q_ref, k_ref, v_ref, km_ref, vm_ref, g_ref, o_ref, m_ref, l_ref, acc_ref,
                 *, t, out_scale):
    qi = pl.program_id(2)
    q = q_ref[...]
    lane = lax.broadcasted_iota(I32, q.shape, 1)
    zero = jnp.zeros_like(q)
    qc = (jnp.where(lane < D_HEAD, q, zero), jnp.where(lane >= D_HEAD, q, zero))
    nt = (((1,), (1,)), ((), ()))
    rep = t // LANES

    km, vm = km_ref[...], vm_ref[...]
    for c in range(2):
        s = lax.dot_general(qc[c], km, nt, preferred_element_type=F32)
        m = jnp.max(s, axis=1, keepdims=True)
        p = jnp.exp(s - m)
        m_ref[c] = jnp.broadcast_to(m, (t, LANES))
        l_ref[c] = jnp.broadcast_to(jnp.sum(p, axis=1, keepdims=True), (t, LANES))
        acc_ref[c] = jnp.dot(p.astype(BF16), vm, preferred_element_type=F32)

    def step(j, masked):
        start = pl.multiple_of(j * t, t)
        kblk = k_ref[pl.ds(start, t), :]
        vblk = v_ref[pl.ds(start, t), :]
        for c in range(2):
            s = lax.dot_general(qc[c], kblk, nt, preferred_element_type=F32)
            if masked:
                row = lax.broadcasted_iota(I32, s.shape, 0)
                col = lax.broadcasted_iota(I32, s.shape, 1)
                s = jnp.where(row >= col, s, -jnp.inf)
            m_prev = m_ref[c]
            m_new = jnp.maximum(m_prev, jnp.max(s, axis=1, keepdims=True))
            alpha = jnp.exp(m_prev - m_new)
            p = jnp.exp(s - jnp.concatenate([m_new] * rep, axis=1))
            l_ref[c] = alpha * l_ref[c] + jnp.sum(p, axis=1, keepdims=True)
            acc_ref[c] = alpha * acc_ref[c] + jnp.dot(p.astype(BF16), vblk, preferred_element_type=F32)
            m_ref[c] = m_new

    def body(j, carry):
        step(j, False)
        return carry

    lax.fori_loop(0, qi, body, 0)
    step(qi, True)

    o = acc_ref[0] / l_ref[0] - lam_ref[0, 0] * (acc_ref[1] / l_ref[1])
    ms = jnp.mean(o * o, axis=1, keepdims=True)
    o_ref[...] = ((o * lax.rsqrt(ms + RMS_EPS)) * g_ref[...] * out_scale).astype(BF16)


def _attn_prompt(lam, q, kb, vb, kmb, vmb, subln_g, batch, seq, out_scale):
    t = ATTN_T
    nq = seq // t
    blk = lambda b, h, i: (b * nq + i, h)
    return pl.pallas_call(
        functools.partial(_attn_kernel, t=t, out_scale=out_scale),
        grid=(batch, N_HEADS, nq),
        in_specs=[pl.BlockSpec(memory_space=pltpu.SMEM),
                  pl.BlockSpec((t, LANES), blk),
                  pl.BlockSpec((seq, LANES), lambda b, h, i: (b, h)),
                  pl.BlockSpec((seq, LANES), lambda b, h, i: (b, h)),
                  pl.BlockSpec((N_META, LANES), lambda b, h, i: (0, h)),
                  pl.BlockSpec((N_META, LANES), lambda b, h, i: (0, h)),
                  pl.BlockSpec((1, LANES), lambda b, h, i: (0, 0))],
        out_specs=pl.BlockSpec((t, LANES), blk),
        out_shape=jax.ShapeDtypeStruct(q.shape, BF16),
        scratch_shapes=[pltpu.VMEM((2, t, LANES), F32), pltpu.VMEM((2, t, LANES), F32),
                        pltpu.VMEM((2, t, LANES), F32)],
        compiler_params=_cparams(("parallel", "parallel", "arbitrary")),
        name="attn_prompt",
    )(lam, q, kb, vb, kmb, vmb, subln_g)


def _decode_kernel(pt_ref, lam_ref, qbd_ref, kn_ref, vn_ref, g_ref, k_hbm, v_hbm,
                   o_ref, m_ref, l_ref, acc_ref, kbuf, vbuf, sem, *, out_scale):
    b, j = pl.program_id(0), pl.program_id(1)
    n_b, n_j = pl.num_programs(0), pl.num_programs(1)
    n_pages = kbuf.shape[1]
    n_hc = qbd_ref.shape[0]
    width = qbd_ref.shape[1]
    rep = width // LANES
    step = b * n_j + j
    slot = step % 2

    def issue(bb, jj, s):
        for i in range(n_pages):
            page = pt_ref[bb, jj * n_pages + i]
            pltpu.make_async_copy(k_hbm.at[page], kbuf.at[s, i], sem.at[0, s]).start()
            pltpu.make_async_copy(v_hbm.at[page], vbuf.at[s, i], sem.at[1, s]).start()

    @pl.when(step == 0)
    def _():
        issue(0, 0, 0)

    @pl.when(step + 1 < n_b * n_j)
    def _():
        wrap = j + 1 == n_j
        issue(jnp.where(wrap, b + 1, b), jnp.where(wrap, 0, j + 1), 1 - slot)

    pltpu.make_async_copy(k_hbm.at[pl.ds(0, n_pages)], kbuf.at[slot], sem.at[0, slot]).wait()
    pltpu.make_async_copy(v_hbm.at[pl.ds(0, n_pages)], vbuf.at[slot], sem.at[1, slot]).wait()

    qf = qbd_ref[...]
    qb = qf.astype(BF16)

    @pl.when(j == 0)
    def _():
        m_ref[...] = jnp.full(m_ref.shape, -jnp.inf, F32)
        l_ref[...] = jnp.zeros(l_ref.shape, F32)
        acc_ref[...] = jnp.zeros(acc_ref.shape, F32)

    ss = [jnp.dot(qb, kbuf[slot, i].astype(BF16), preferred_element_type=F32)
          for i in range(n_pages)]
    m_cur = ss[0].max(axis=1, keepdims=True)
    for s in ss[1:]:
        m_cur = jnp.maximum(m_cur, s.max(axis=1, keepdims=True))
    m_prev = m_ref[...]
    m_new = jnp.maximum(m_prev, m_cur)
    alpha = jnp.exp(m_prev - m_new)
    lsum = jnp.zeros((n_hc, 1), F32)
    pv = jnp.zeros((n_hc, width), F32)
    for i in range(n_pages):
        p = jnp.exp(ss[i] - m_new)
        lsum = lsum + jnp.sum(p, axis=1, keepdims=True)
        pb = p.astype(BF16)
        pv = pv + jnp.concatenate(
            [jnp.dot(pb, vbuf[slot, i, pl.ds(h, PAGE_SIZE, stride=N_HEADS), :].astype(BF16),
                     preferred_element_type=F32) for h in range(N_HEADS)], axis=1)
    l_new = alpha * l_ref[...] + lsum
    acc_new = jnp.concatenate([alpha] * rep, axis=1) * acc_ref[...] + pv
    m_ref[...] = m_new
    l_ref[...] = l_new
    acc_ref[...] = acc_new

    @pl.when(j == pl.num_programs(1) - 1)
    def _():
        s_new = jnp.sum(qf * kn_ref[...], axis=1, keepdims=True)
        m_f = jnp.maximum(m_new, s_new)
        a2 = jnp.exp(m_new - m_f)
        p_new = jnp.exp(s_new - m_f)
        l_f = a2 * l_new + p_new
        acc_f = jnp.concatenate([a2] * rep, axis=1) * acc_new + jnp.concatenate([p_new] * rep, axis=1) * vn_ref[...]
        o = acc_f / jnp.concatenate([l_f] * rep, axis=1)
        row = lax.broadcasted_iota(I32, o.shape, 0)
        ln = lax.broadcasted_iota(I32, o.shape, 1)
        own = (ln // D_VHEAD) == (row // 2)
        coef = jnp.where(row % 2 == 0, 1.0, -lam_ref[0, 0])
        od = jnp.sum(jnp.where(own, o * coef, 0.0), axis=0, keepdims=True)
        parts = []
        for h in range(rep):
            oh = od[:, h * D_VHEAD:(h + 1) * D_VHEAD]
            ms = jnp.mean(oh * oh, axis=1, keepdims=True)
            parts.append(oh * lax.rsqrt(ms + RMS_EPS))
        o_ref[...] = jnp.concatenate(parts, axis=1) * g_ref[...] * out_scale


def _attn_decode(page_table, lam, qbd, k_new, v_new, g_row, cache_k, cache_v, out_scale):
    bd, n_hc, width = qbd.shape
    n_steps = page_table.shape[1] // DEC_PAGES
    row_spec = pl.BlockSpec((None, 1, width), lambda b, j, pt: (b, 0, 0))
    grid_spec = pltpu.PrefetchScalarGridSpec(
        num_scalar_prefetch=1,
        grid=(bd, n_steps),
        in_specs=[pl.BlockSpec(memory_space=pltpu.SMEM),
                  pl.BlockSpec((None, n_hc, width), lambda b, j, pt: (b, 0, 0)),
                  row_spec, row_spec,
                  pl.BlockSpec((1, width), lambda b, j, pt: (0, 0)),
                  pl.BlockSpec(memory_space=pl.ANY), pl.BlockSpec(memory_space=pl.ANY)],
        out_specs=row_spec,
        scratch_shapes=[pltpu.VMEM((n_hc, LANES), F32), pltpu.VMEM((n_hc, LANES), F32),
                        pltpu.VMEM((n_hc, width), F32),
                        pltpu.VMEM((2, DEC_PAGES) + cache_k.shape[1:], F32),
                        pltpu.VMEM((2, DEC_PAGES) + cache_v.shape[1:], F32),
                        pltpu.SemaphoreType.DMA((2, 2))],
    )
    return pl.pallas_call(
        functools.partial(_decode_kernel, out_scale=out_scale),
        grid_spec=grid_spec,
        out_shape=jax.ShapeDtypeStruct((bd, 1, width), F32),
        compiler_params=_cparams(("arbitrary", "arbitrary")),
        name="attn_decode",
    )(page_table, lam, qbd, k_new, v_new, g_row, cache_k, cache_v)


def _ln_swish(y, g, b):
    mu = jnp.mean(y, axis=-1, keepdims=True)
    var = jnp.mean(jnp.square(y - mu), axis=-1, keepdims=True)
    z = (y - mu) * lax.rsqrt(var + LN_EPS) * g + b
    return z * jax.nn.sigmoid(z)


def _conv_prompt_kernel(cur_ref, prev_ref, hist_ref, w_ref, b_ref, g_ref, beta_ref, o_ref, buf_ref):
    i = pl.program_id(1)
    t = cur_ref.shape[0]
    buf_ref[0:CONV_HIST, :] = jnp.where(i == 0, hist_ref[...], prev_ref[...])
    buf_ref[CONV_HIST:CONV_HIST + t, :] = cur_ref[...]
    off = CONV_HIST - (CONV_WIDTH - 1)

    for r0 in range(0, t, CONV_CHUNK):
        acc = jnp.zeros((CONV_CHUNK, cur_ref.shape[1]), F32)
        for w in range(CONV_WIDTH):
            acc = acc + buf_ref[r0 + off + w:r0 + off + w + CONV_CHUNK, :] * w_ref[w:w + 1, :]
        y = acc + b_ref[...]
        o_ref[r0:r0 + CONV_CHUNK, :] = _ln_swish(y, g_ref[...], beta_ref[...]).astype(o_ref.dtype)


def _conv_prompt(glu, hist, w, b, g, beta):
    batch, seq, c = glu.shape
    t = CONV_T
    per = t // CONV_HIST
    vec = pl.BlockSpec((1, c), lambda bb, i: (0, 0))
    return pl.pallas_call(
        _conv_prompt_kernel,
        grid=(batch, seq // t),
        in_specs=[pl.BlockSpec((None, t, c), lambda bb, i: (bb, i, 0)),
                  pl.BlockSpec((None, CONV_HIST, c), lambda bb, i: (bb, jnp.maximum(i * per - 1, 0), 0)),
                  pl.BlockSpec((CONV_HIST, c), lambda bb, i: (0, 0)),
                  pl.BlockSpec(w.shape, lambda bb, i: (0, 0)), vec, vec, vec],
        out_specs=pl.BlockSpec((None, t, c), lambda bb, i: (bb, i, 0)),
        out_shape=jax.ShapeDtypeStruct(glu.shape, BF16),
        scratch_shapes=[pltpu.VMEM((CONV_HIST + t, c), F32)],
        compiler_params=_cparams(("parallel", "parallel")),
        name="conv_prompt",
    )(glu, glu, hist, w, b, g, beta)


def _conv_sample_kernel(st_ref, glu_ref, w_ref, b_ref, g_ref, beta_ref, o_ref):
    n_hist = st_ref.shape[0]
    y = glu_ref[...] * w_ref[n_hist:n_hist + 1, :] + b_ref[...]
    for w in range(n_hist):
        y = y + st_ref[w] * w_ref[w:w + 1, :]
    o_ref[...] = _ln_swish(y, g_ref[...], beta_ref[...]).astype(o_ref.dtype)


def _conv_sample(state, glu, w, b, g, beta):
    n_hist, bd, c = state.shape
    tb = 16
    vec = pl.BlockSpec((1, c), lambda i: (0, 0))
    return pl.pallas_call(
        _conv_sample_kernel,
        grid=(bd // tb,),
        in_specs=[pl.BlockSpec((n_hist, tb, c), lambda i: (0, i, 0)),
                  pl.BlockSpec((tb, c), lambda i: (i, 0)),
                  pl.BlockSpec(w.shape, lambda i: (0, 0)), vec, vec, vec],
        out_specs=pl.BlockSpec((tb, c), lambda i: (i, 0)),
        out_shape=jax.ShapeDtypeStruct((bd, c), BF16),
        compiler_params=_cparams(("parallel",)),
        name="conv_sample",
    )(state, glu, w, b, g, beta)


def _post_rows(n, oa, yc, x, wo_ref, g2_ref, wrh_ref, wrl_ref, br_ref, tri_ref,
               x1_ref, ht_ref, route_ref, run_ref):
    half = oa.shape[1]
    x1 = (x + jnp.dot(oa, wo_ref[0:half, :], preferred_element_type=F32)
          + jnp.dot(yc, wo_ref[half:2 * half, :], preferred_element_type=F32))
    x1_ref[0:n, :] = x1
    ms = jnp.mean(x1 * x1, axis=-1, keepdims=True)
    h = (x1 * lax.rsqrt(ms + RMS_EPS)) * g2_ref[...]
    for s in range(ROW_TILE):
        ht_ref[pl.ds(s, n, stride=ROW_TILE), :] = h[:, s * LANES:(s + 1) * LANES]

    h_hi = h.astype(BF16)
    h_lo = (h - h_hi.astype(F32)).astype(BF16)
    logits = (jnp.dot(h_hi, wrh_ref[...], preferred_element_type=F32)
              + jnp.dot(h_lo, wrh_ref[...], preferred_element_type=F32)
              + jnp.dot(h_hi, wrl_ref[...], preferred_element_type=F32)) + br_ref[...]
    lane = lax.broadcasted_iota(I32, logits.shape, 1)
    work = jnp.where(lane < N_EXPERTS, logits, -jnp.inf)
    vals, idxs = [], []
    onehot = jnp.zeros(logits.shape, F32)
    for _ in range(TOP_K):
        m = jnp.max(work, axis=1, keepdims=True)
        idx = jnp.min(jnp.where(work == m, lane, LANES), axis=1, keepdims=True)
        sel = lane == idx
        vals.append(m)
        idxs.append(idx)
        onehot = jnp.where(sel, 1.0, onehot)
        work = jnp.where(sel, -jnp.inf, work)
    exps = [jnp.exp(v - vals[0]) for v in vals]
    denom = exps[0] + exps[1] + exps[2] + exps[3]
    gates = [e / denom for e in exps]

    before = jnp.dot(tri_ref[0:n, 0:n], onehot.astype(BF16), preferred_element_type=F32) + run_ref[...]
    route = jnp.zeros(logits.shape, F32)
    for k in range(TOP_K):
        rank = jnp.sum(jnp.where(lane == idxs[k], before, 0.0), axis=1, keepdims=True)
        route = jnp.where(lane == k, idxs[k].astype(F32), route)
        route = jnp.where(lane == TOP_K + k, rank, route)
        route = jnp.where(lane == 2 * TOP_K + k, gates[k], route)
    route_ref[0:n, :] = route
    run_ref[...] = run_ref[...] + jnp.sum(onehot, axis=0, keepdims=True)


def _post_kernel(oap_ref, ycp_ref, xp_ref, oas_ref, ycs_ref, xs_ref,
                 wo_ref, g2_ref, wrh_ref, wrl_ref, br_ref, tri_ref,
                 x1_ref, ht_ref, route_ref, cnt_ref, run_ref):
    i = pl.program_id(0)
    last = pl.num_programs(0) - 1
    tm = xp_ref.shape[0]
    ns = xs_ref.shape[0]
    shared = (wo_ref, g2_ref, wrh_ref, wrl_ref, br_ref, tri_ref, x1_ref, ht_ref, route_ref, run_ref)

    @pl.when(i == 0)
    def _():
        run_ref[...] = jnp.zeros(run_ref.shape, F32)

    @pl.when(i < last)
    def _():
        _post_rows(tm, oap_ref[...], ycp_ref[...], xp_ref[...], *shared)

    @pl.when(i == last)
    def _():
        _post_rows(ns, oas_ref[...], ycs_ref[...], xs_ref[...], *shared)
        x1_ref[ns:tm, :] = jnp.zeros((tm - ns, x1_ref.shape[1]), F32)
        route_ref[ns:tm, :] = jnp.zeros((tm - ns, LANES), F32)
        ht_ref[ns * ROW_TILE:tm * ROW_TILE, :] = jnp.zeros(((tm - ns) * ROW_TILE, LANES), F32)
        cnt_ref[...] = run_ref[...]


def _post(oa_p, yc_p, x_p, oa_s, yc_s, x_s, wo_bf, g2, wr_hi, wr_lo, br):
    rows, d = x_p.shape
    ns = x_s.shape[0]
    tm = POST_TM
    n_steps = rows // tm + 1
    n_pad = n_steps * tm
    half = oa_p.shape[1]
    tri = (np.arange(tm)[:, None] > np.arange(tm)[None, :]).astype(np.float32)
    tri = jnp.asarray(tri, BF16)
    last_p = rows // tm - 1
    p_spec = lambda w: pl.BlockSpec((tm, w), lambda i: (jnp.minimum(i, last_p), 0))
    full = lambda a: pl.BlockSpec(a.shape, lambda i: (0, 0))
    out_spec = lambda r, w: pl.BlockSpec((r, w), lambda i: (i, 0))
    return pl.pallas_call(
        _post_kernel,
        grid=(n_steps,),
        in_specs=[p_spec(half), p_spec(half), p_spec(d), full(oa_s), full(yc_s), full(x_s),
                  full(wo_bf), full(g2), full(wr_hi), full(wr_lo), full(br), full(tri)],
        out_specs=(out_spec(tm, d), out_spec(tm * ROW_TILE, LANES), out_spec(tm, LANES),
                   pl.BlockSpec((1, LANES), lambda i: (0, 0))),
        out_shape=(jax.ShapeDtypeStruct((n_pad, d), F32), jax.ShapeDtypeStruct((n_pad * ROW_TILE, LANES), F32),
                   jax.ShapeDtypeStruct((n_pad, LANES), F32), jax.ShapeDtypeStruct((1, LANES), F32)),
        scratch_shapes=[pltpu.VMEM((1, LANES), F32)],
        compiler_params=_cparams(("arbitrary",)),
        name="post",
    )(oa_p, yc_p, x_p, oa_s, yc_s, x_s, wo_bf, g2, wr_hi, wr_lo, br, tri)


def _moe_kernel(be_ref, nv_ref, tok_cur_ref, tok_nxt_ref, h_hbm,
                wg_ref, wu_ref, wd_ref, bg_ref, bu_ref, bd_ref, ys_ref,
                xbuf, sem, wgb, wub, wdb):
    i = pl.program_id(0)
    nv = nv_ref[0]
    tb = xbuf.shape[1] // ROW_TILE
    slot = i % 2

    def issue(row_ref, s):
        for r in range(tb):
            src = h_hbm.at[pl.ds(pl.multiple_of(row_ref[r], ROW_TILE), ROW_TILE)]
            pltpu.make_async_copy(src, xbuf.at[s, pl.ds(r * ROW_TILE, ROW_TILE)],
                                  sem.at[s]).start(priority=r % N_DMA_QUEUES)

    def wait(s):
        pltpu.make_async_copy(h_hbm.at[pl.ds(0, tb * ROW_TILE)], xbuf.at[s], sem.at[s]).wait()

    @pl.when(i == 0)
    def _():
        issue(tok_cur_ref, 0)

    @pl.when(i + 1 < nv)
    def _():
        issue(tok_nxt_ref, 1 - slot)

    @pl.when(i < nv)
    def _():
        changed = jnp.logical_or(i == 0, be_ref[i] != be_ref[jnp.maximum(i - 1, 0)])

        @pl.when(changed)
        def _():
            wgb[...] = wg_ref[...].astype(BF16)
            wub[...] = wu_ref[...].astype(BF16)
            wdb[...] = wd_ref[...].astype(BF16)

        wait(slot)
        x = jnp.concatenate([xbuf[slot, pl.ds(s, tb, stride=ROW_TILE), :] for s in range(ROW_TILE)],
                            axis=1).astype(BF16)

        g = jnp.minimum(jnp.dot(x, wgb[...], preferred_element_type=F32) + bg_ref[...], SWIGLU_LIMIT)
        u = jnp.clip(jnp.dot(x, wub[...], preferred_element_type=F32) + bu_ref[...], -SWIGLU_LIMIT, SWIGLU_LIMIT)
        a = (g * jax.nn.sigmoid(SWIGLU_ALPHA * g) * (u + 1.0)).astype(BF16)
        y = jnp.dot(a, wdb[...], preferred_element_type=F32) + bd_ref[...]
        for s in range(ROW_TILE):
            ys_ref[pl.ds(s, tb, stride=ROW_TILE), :] = y[:, s * LANES:(s + 1) * LANES]

    @pl.when(i >= nv)
    def _():
        ys_ref[...] = jnp.zeros(ys_ref.shape, F32)


def _moe(block_e, n_valid, slot_row, h_tiles, wg, wu, wd, bg, bu, bd):
    p = slot_row.shape[0]
    tb = MOE_TB
    nb = p // tb
    n_e, d, de = wg.shape
    assert d == ROW_TILE * LANES
    w_spec = lambda r, c: pl.BlockSpec((None, r, c), lambda i, be, nv: (be[i], 0, 0))
    b_spec = lambda c: pl.BlockSpec((None, 1, c), lambda i, be, nv: (be[i], 0, 0))
    last = lambda i, nv: jnp.minimum(i, nv[0] - 1)
    grid_spec = pltpu.PrefetchScalarGridSpec(
        num_scalar_prefetch=2,
        grid=(nb,),
        in_specs=[pl.BlockSpec((tb,), lambda i, be, nv: (last(i, nv),), memory_space=pltpu.SMEM),
                  pl.BlockSpec((tb,), lambda i, be, nv: (last(i + 1, nv),), memory_space=pltpu.SMEM),
                  pl.BlockSpec(memory_space=pl.ANY),
                  w_spec(d, de), w_spec(d, de), w_spec(de, d), b_spec(de), b_spec(de), b_spec(d)],
        out_specs=pl.BlockSpec((tb * ROW_TILE, LANES), lambda i, be, nv: (i, 0)),
        scratch_shapes=[pltpu.VMEM((2, tb * ROW_TILE, LANES), F32), pltpu.SemaphoreType.DMA((2,)),
                        pltpu.VMEM((d, de), BF16), pltpu.VMEM((d, de), BF16), pltpu.VMEM((de, d), BF16)],
    )
    return pl.pallas_call(
        _moe_kernel,
        grid_spec=grid_spec,
        out_shape=jax.ShapeDtypeStruct((p * ROW_TILE, LANES), F32),
        compiler_params=_cparams(("arbitrary",)),
        name="moe",
    )(block_e, n_valid, slot_row, slot_row, h_tiles, wg, wu, wd,
      bg.reshape(n_e, 1, de), bu.reshape(n_e, 1, de), bd.reshape(n_e, 1, d))


def _combine_kernel(pos_cur_ref, pos_nxt_ref, ys_hbm, x1_ref, route_ref, fg_ref, yp_ref, ys_ref, buf, sem):
    i = pl.program_id(0)
    n = pl.num_programs(0)
    t = x1_ref.shape[0]
    slot = i % 2

    def issue(pos_ref, s):
        for r in range(t):
            for k in range(TOP_K):
                src = ys_hbm.at[pl.ds(pl.multiple_of(pos_ref[r * TOP_K + k], ROW_TILE), ROW_TILE)]
                pltpu.make_async_copy(src, buf.at[s, k, pl.ds(r * ROW_TILE, ROW_TILE)],
                                      sem.at[s]).start(priority=k % N_DMA_QUEUES)

    def wait(s):
        for k in range(TOP_K):
            pltpu.make_async_copy(ys_hbm.at[pl.ds(0, t * ROW_TILE)], buf.at[s, k], sem.at[s]).wait()

    @pl.when(i == 0)
    def _():
        issue(pos_cur_ref, 0)

    issue(pos_nxt_ref, 1 - slot)
    wait(slot)
    route = route_ref[...]
    gates = [route[:, 2 * TOP_K + k:2 * TOP_K + k + 1] for k in range(TOP_K)]
    parts = []
    for s in range(ROW_TILE):
        acc = x1_ref[:, s * LANES:(s + 1) * LANES]
        for k in range(TOP_K):
            acc = acc + gates[k] * buf[slot, k, pl.ds(s, t, stride=ROW_TILE), :]
        parts.append(acc)
    y = jnp.concatenate(parts, axis=1)
    ms = jnp.mean(y * y, axis=-1, keepdims=True)
    out = (y * lax.rsqrt(ms + RMS_EPS)) * fg_ref[...]

    @pl.when(i < n - 1)
    def _():
        yp_ref[...] = out

    @pl.when(i == n - 1)
    def _():
        ys_ref[...] = out
        wait(1 - slot)


def _combine(pos_row, ys_tiles, x1, route, fg, n_prompt, n_sample):
    d = x1.shape[1]
    t = COMB_T
    assert n_sample == t and n_prompt % t == 0
    n = n_prompt // t + 1
    return pl.pallas_call(
        _combine_kernel,
        grid=(n,),
        in_specs=[pl.BlockSpec((t * TOP_K,), lambda i: (i,), memory_space=pltpu.SMEM),
                  pl.BlockSpec((t * TOP_K,), lambda i: (jnp.minimum(i + 1, n - 1),), memory_space=pltpu.SMEM),
                  pl.BlockSpec(memory_space=pl.ANY),
                  pl.BlockSpec((t, d), lambda i: (i, 0)),
                  pl.BlockSpec((t, LANES), lambda i: (i, 0)),
                  pl.BlockSpec((1, d), lambda i: (0, 0))],
        out_specs=(pl.BlockSpec((t, d), lambda i: (jnp.minimum(i, n - 2), 0)),
                   pl.BlockSpec((t, d), lambda i: (0, 0))),
        out_shape=(jax.ShapeDtypeStruct((n_prompt, d), F32), jax.ShapeDtypeStruct((n_sample, d), F32)),
        scratch_shapes=[pltpu.VMEM((2, TOP_K, t * ROW_TILE, LANES), F32), pltpu.SemaphoreType.DMA((2,))],
        compiler_params=_cparams(("arbitrary",)),
        name="combine",
    )(pos_row, pos_row, ys_tiles, x1, route, fg)


def _invert_kernel(pos_ref, init_ref, out_ref):
    pltpu.sync_copy(init_ref, out_ref)
    n = pos_ref.shape[0]

    def body(j, carry):
        base = j * INVERT_UNROLL
        row0 = base * (ROW_TILE // TOP_K)
        for u in range(INVERT_UNROLL):
            out_ref[pos_ref[base + u]] = row0 + (u // TOP_K) * ROW_TILE
        return carry

    lax.fori_loop(0, n // INVERT_UNROLL, body, 0)


def _invert(pos_flat, fill_row, n_slots):
    init = jnp.full((n_slots,), fill_row, I32)
    smem = pl.BlockSpec(memory_space=pltpu.SMEM)
    return pl.pallas_call(
        _invert_kernel,
        in_specs=[smem, smem],
        out_specs=smem,
        out_shape=jax.ShapeDtypeStruct((n_slots,), I32),
        name="invert",
    )(pos_flat, init)


def _routing_tables(route, counts, n_tok):
    tb = MOE_TB
    a = n_tok * TOP_K
    idx = route[:, 0:TOP_K].astype(I32)
    rank = route[:, TOP_K:2 * TOP_K].astype(I32)
    cnt = counts[0, :N_EXPERTS].astype(I32)
    padded = ((cnt + tb - 1) // tb) * tb
    ends = jnp.cumsum(padded)
    pad_start = ends - padded
    onehot = idx[..., None] == jnp.arange(N_EXPERTS, dtype=I32)
    pos = jnp.sum(jnp.where(onehot, pad_start, 0), axis=-1) + rank
    nb = a // tb + N_EXPERTS
    pos_flat = pos.reshape(a)
    slot_row = _invert(pos_flat, n_tok * ROW_TILE, nb * tb)
    block_start = jnp.arange(nb, dtype=I32) * tb
    n_valid = (ends[-1] // tb).astype(I32)
    be = jnp.sum(block_start[:, None] >= ends[None, :], axis=1).astype(I32)
    be_last = jnp.sum(jnp.where(jnp.arange(nb) == n_valid - 1, be, 0))
    block_e = jnp.where(jnp.arange(nb) < n_valid, jnp.minimum(be, N_EXPERTS - 1), be_last).astype(I32)
    return pos_flat * ROW_TILE, slot_row, block_e, n_valid.reshape(1)


def kernel(x_prompt, x_sample, cache_k, cache_v, page_table, state_conv, meta_tokens, norm1_g, w_in,
           lambda_q1, lambda_k1, lambda_q2, lambda_k2, subln_g, conv_w, conv_b, conv_ln_g, conv_ln_b,
           w_out, norm2_g, w_router, b_router, w_gate, b_gate, w_up, b_up, w_down, b_down, final_norm_g):
    depth = norm1_g.shape[0]
    assert depth == 1, "single trunk layer"
    layer = 0
    batch, seq, d = x_prompt.shape
    bd, dec_seq, _ = x_sample.shape
    assert dec_seq == 1
    past_len = page_table.shape[1] * PAGE_SIZE
    qk_w = N_HEADS * 2 * D_HEAD
    v_w = N_HEADS * D_VHEAD
    c_conv = conv_w.shape[2]
    out_scale = 1.0 - _lambda_init(layer)

    lam = (jnp.exp(jnp.sum(lambda_q1[layer] * lambda_k1[layer]))
           - jnp.exp(jnp.sum(lambda_q2[layer] * lambda_k2[layer])) + _lambda_init(layer)).reshape(1, 1).astype(F32)
    w_in_bf = w_in[layer].astype(BF16)
    g1 = norm1_g[layer].reshape(1, d)

    tab_p = _rope_tables(np.arange(N_META, N_META + seq))
    q_p, k_p, kb_p, v_p, vb_p, glu_p = _front(x_prompt.reshape(batch * seq, d), g1, w_in_bf, tab_p,
                                              FRONT_TM, seq // FRONT_TM)
    x_sm = jnp.concatenate([x_sample.reshape(bd, d), meta_tokens.astype(F32)], axis=0)
    tab_sm = _rope_tables(np.concatenate([np.full(bd, past_len), np.arange(N_META)]))
    q_sm, k_sm, kb_sm, v_sm, vb_sm, glu_sm = _front(x_sm, g1, w_in_bf, tab_sm, bd + N_META, 1)

    sub_g = subln_g[layer].reshape(1, D_VHEAD)
    oa_p = _attn_prompt(lam, q_p, kb_p, vb_p, kb_sm[bd:], vb_sm[bd:], sub_g, batch, seq, out_scale)

    n_hc = 2 * N_HEADS
    hc_mask = (jnp.arange(qk_w)[None, :] // D_HEAD) == jnp.arange(n_hc)[:, None]
    qbd = jnp.where(hc_mask[None], q_sm[:bd].astype(F32)[:, None, :], 0.0)
    n_phys = cache_k.shape[1]
    kt = jnp.transpose(cache_k[layer], (0, 2, 3, 4, 1)).reshape(n_phys, qk_w, PAGE_SIZE)
    vr = cache_v[layer].reshape(n_phys, PAGE_SIZE * N_HEADS, D_VHEAD)
    oa_s = _attn_decode(page_table, lam, qbd, k_sm[:bd].reshape(bd, 1, qk_w), v_sm[:bd].reshape(bd, 1, v_w),
                        jnp.tile(sub_g, (1, N_HEADS)), kt, vr, out_scale)

    cw = jnp.concatenate([conv_w[layer], jnp.zeros((CONV_HIST - CONV_WIDTH, c_conv), F32)], axis=0)
    cb, cg, cbeta = conv_b[layer].reshape(1, -1), conv_ln_g[layer].reshape(1, -1), conv_ln_b[layer].reshape(1, -1)
    hist = jnp.concatenate([jnp.zeros((CONV_HIST - N_META, c_conv), F32), glu_sm[bd:]], axis=0)
    yc_p = _conv_prompt(glu_p.reshape(batch, seq, c_conv), hist, cw, cb, cg, cbeta)
    yc_s = _conv_sample(jnp.transpose(state_conv[layer], (1, 0, 2)), glu_sm[:bd], cw, cb, cg, cbeta)

    n_prompt = batch * seq
    n_tok = n_prompt + bd
    wr = jnp.zeros((d, LANES), F32).at[:, :N_EXPERTS].set(w_router[layer])
    wr_hi = wr.astype(BF16)
    wr_lo = (wr - wr_hi.astype(F32)).astype(BF16)
    br = jnp.zeros((1, LANES), F32).at[0, :N_EXPERTS].set(b_router[layer])
    x1, h_tiles, route, counts = _post(oa_p, yc_p.reshape(n_prompt, c_conv), x_prompt.reshape(n_prompt, d),
                                       oa_s.reshape(bd, v_w).astype(BF16), yc_s, x_sample.reshape(bd, d),
                                       w_out[layer].astype(BF16), norm2_g[layer].reshape(1, d), wr_hi, wr_lo, br)

    pos_row, slot_row, block_e, n_valid = _routing_tables(route[:n_tok], counts, n_tok)
    ys_tiles = _moe(block_e, n_valid, slot_row, h_tiles, w_gate[layer], w_up[layer], w_down[layer],
                    b_gate[layer], b_up[layer], b_down[layer])
    y_p, y_s = _combine(pos_row, ys_tiles, x1, route, final_norm_g.reshape(1, d), n_prompt, bd)

    y_prompt = y_p.reshape(batch, seq, d)
    y_sample = y_s.reshape(bd, 1, d)
    k_meta = jnp.broadcast_to(k_sm[bd:][None], (batch, N_META, qk_w))
    v_meta = jnp.broadcast_to(v_sm[bd:][None], (batch, N_META, v_w))
    k_prompt = jnp.concatenate([k_meta, k_p.reshape(batch, seq, qk_w)], axis=1)
    v_prompt = jnp.concatenate([v_meta, v_p.reshape(batch, seq, v_w)], axis=1)
    k_prompt = k_prompt.reshape(1, batch, seq + N_META, N_HEADS, 2, D_HEAD)
    v_prompt = v_prompt.reshape(1, batch, seq + N_META, N_HEADS, D_VHEAD)
    n_keep = CONV_WIDTH - 1
    conv_prompt = glu_p.reshape(batch, seq, c_conv)[:, seq - n_keep:][None]
    k_sample = k_sm[:bd].reshape(1, bd, 1, N_HEADS, 2, D_HEAD)
    v_sample = v_sm[:bd].reshape(1, bd, 1, N_HEADS, D_VHEAD)
    conv_sample = jnp.concatenate([state_conv[layer][:, 1:], glu_sm[:bd][:, None, :]], axis=1)[None]
    return (y_prompt, y_sample, k_prompt, v_prompt, conv_prompt, k_sample, v_sample, conv_sample)
```
